```python
import jax, jax.numpy as jnp
from jax import lax
import numpy as np

D_MODEL = 2048
BATCH = 8
SEQ = 4096
DEPTH = 1
DEC_BATCH = 32
DEC_SEQ = 16
PAST_LEN = 1024

CHUNK = 64
ROPE_THETA = 500000.0
NORM_EPS = 1e-6
Q_BLOCK = 128
NEG_INF = -1e30
MLA_HEADS = 8
Q_LORA = 512
KV_LORA = 256
MLA_NOPE = 128
MLA_ROPE = 64
MLA_V = 128
DSA_HEADS = 8
DSA_KV_HEADS = 2
DSA_HD = 128
DSA_ROT = DSA_HD // 4
IDX_HEADS = 16
IDX_DIM = 64
IDX_ROT = IDX_DIM // 4
TOPK_MAX = 256
D_FF = 5632
CONV_W = 3

IN_WIDTHS = (Q_LORA, KV_LORA + MLA_ROPE, DSA_HEADS * DSA_HD, 2 * DSA_KV_HEADS * DSA_HD,
             IDX_HEADS * IDX_DIM, IDX_DIM, IDX_HEADS, 2 * D_MODEL)
IN_COLS = sum(IN_WIDTHS)
IN_SPLITS = tuple(sum(IN_WIDTHS[:i + 1]) for i in range(len(IN_WIDTHS) - 1))

kernel_name = 'mla_dsa_gated_convffn_stream_step'


def rms_norm(x, g):
    xf = x.astype(jnp.float32)
    y = xf * lax.rsqrt(jnp.mean(xf * xf, axis=-1, keepdims=True) + NORM_EPS)
    return (y * g.astype(jnp.float32)).astype(x.dtype)


def rope(x, pos):
    half = x.shape[-1] // 2
    inv = ROPE_THETA ** (-jnp.arange(half, dtype=jnp.float32) / half)
    ang = pos.astype(jnp.float32)[:, None] * inv[None, :]
    shape = (ang.shape[0],) + (1,) * (x.ndim - 3) + (half,)
    cos = jnp.cos(ang).reshape(shape)
    sin = jnp.sin(ang).reshape(shape)
    xf = x.astype(jnp.float32)
    x1, x2 = xf[..., :half], xf[..., half:]
    return jnp.concatenate([x1 * cos - x2 * sin, x2 * cos + x1 * sin], axis=-1).astype(x.dtype)


def partial_rope(x, pos, rot):
    return jnp.concatenate([rope(x[..., :rot], pos), x[..., rot:]], axis=-1)


def chunk_visible(qpos, kpos):
    return (kpos // CHUNK)[None, :] <= (qpos // CHUNK)[:, None]


def to_blocks(a, blk):
    b, t = a.shape[:2]
    return jnp.moveaxis(a.reshape((b, t // blk, blk) + a.shape[2:]), 1, 0)


def from_blocks(a):
    a = jnp.moveaxis(a, 0, 1)
    return a.reshape((a.shape[0], a.shape[1] * a.shape[2]) + a.shape[3:])


def mla_attention(q, k, v, qpos, kpos):
    scale = (MLA_NOPE + MLA_ROPE) ** -0.5
    blk = min(Q_BLOCK, q.shape[1])

    def one(args):
        qb, pb = args
        s = jnp.einsum('bqhd,bkhd->bhqk', qb, k).astype(jnp.float32) * scale
        s = jnp.where(chunk_visible(pb, kpos)[None, None], s, NEG_INF)
        p = jax.nn.softmax(s, axis=-1).astype(v.dtype)
        return jnp.einsum('bhqk,bkhd->bqhd', p, v)

    return from_blocks(lax.map(one, (to_blocks(q, blk), qpos.reshape(-1, blk))))


def dsa_attention(q, qi, wi, k, v, ki, qpos, kpos):
    topk = min(TOPK_MAX, k.shape[1] // 4)
    blk = min(Q_BLOCK, q.shape[1])
    gather = jax.vmap(lambda rows, idx: rows[idx])
    rep = DSA_HEADS // DSA_KV_HEADS

    def one(args):
        qb, qib, wib, pb = args
        dots = jnp.einsum('bqhd,bkd->bqhk', qib, ki).astype(jnp.float32) * (IDX_DIM ** -0.5)
        score = jnp.einsum('bqh,bqhk->bqk', wib.astype(jnp.float32), jax.nn.relu(dots))
        score = jnp.where(chunk_visible(pb, kpos)[None], score, NEG_INF)
        _, sel = lax.top_k(score, topk)
        valid = (kpos[sel] // CHUNK) <= (pb // CHUNK)[None, :, None]
        ks = gather(k, sel)
        vs = gather(v, sel)
        qg = qb.reshape(qb.shape[:2] + (DSA_KV_HEADS, rep, DSA_HD))
        s = jnp.einsum('bqgrd,bqkgd->bqgrk', qg, ks).astype(jnp.float32) * (DSA_HD ** -0.5)
        s = jnp.where(valid[:, :, None, None, :], s, NEG_INF)
        p = jax.nn.softmax(s, axis=-1).astype(vs.dtype)
        o = jnp.einsum('bqgrk,bqkgd->bqgrd', p, vs)
        return o.reshape(qb.shape)

    args = (to_blocks(q, blk), to_blocks(qi, blk), to_blocks(wi, blk), qpos.reshape(-1, blk))
    return from_blocks(lax.map(one, args))


def _layer(x, pos, past, prm):
    (attn_norm, w_in, q_a_norm, w_q_up, kv_a_norm, w_kv_up, mla_q_nope_norm, mla_q_rope_norm,
     mla_k_nope_norm, mla_k_rope_norm, dsa_q_norm, dsa_k_norm, w_o_mla, w_o_dsa, w_out,
     ffn_norm, w_ffn_up, conv_w, conv_b, w_ffn_down) = prm
    b, t, _ = x.shape
    h = rms_norm(x, attn_norm)
    c_q, kv_a, q_d, kv_d, qi, ki, wi, gates = jnp.split(h @ w_in, IN_SPLITS, axis=-1)

    q = (rms_norm(c_q, q_a_norm) @ w_q_up).reshape(b, t, MLA_HEADS, MLA_NOPE + MLA_ROPE)
    q_nope = rms_norm(q[..., :MLA_NOPE], mla_q_nope_norm)
    q_pe = rope(rms_norm(q[..., MLA_NOPE:], mla_q_rope_norm), pos)
    c_kv = rms_norm(kv_a[..., :KV_LORA], kv_a_norm)
    k_pe = rope(rms_norm(kv_a[..., KV_LORA:], mla_k_rope_norm), pos)

    q_d = partial_rope(rms_norm(q_d.reshape(b, t, DSA_HEADS, DSA_HD), dsa_q_norm), pos, DSA_ROT)
    k_new, v_new = jnp.split(kv_d.reshape(b, t, 2 * DSA_KV_HEADS, DSA_HD), 2, axis=2)
    k_new = partial_rope(rms_norm(k_new, dsa_k_norm), pos, DSA_ROT)
    qi = partial_rope(qi.reshape(b, t, IDX_HEADS, IDX_DIM), pos, IDX_ROT)
    ki = partial_rope(ki, pos, IDX_ROT)
    wi = wi * (IDX_HEADS ** -0.5)

    if past is None:
        all_ckv, all_kpe, all_k, all_v, all_ki = c_kv, k_pe, k_new, v_new, ki
        conv_hist = jnp.zeros((b, CONV_W - 1, D_FF), x.dtype)
        kpos = pos
    else:
        p_ckv, p_kpe, p_k, p_v, p_ki, conv_hist = past
        all_ckv = jnp.concatenate([p_ckv, c_kv], axis=1)
        all_kpe = jnp.concatenate([p_kpe, k_pe], axis=1)
        all_k = jnp.concatenate([p_k, k_new], axis=1)
        all_v = jnp.concatenate([p_v, v_new], axis=1)
        all_ki = jnp.concatenate([p_ki, ki], axis=1)
        kpos = jnp.arange(all_ckv.shape[1], dtype=jnp.int32)
    s_len = all_ckv.shape[1]

    kv = (all_ckv @ w_kv_up).reshape(b, s_len, MLA_HEADS, MLA_NOPE + MLA_V)
    k_nope = rms_norm(kv[..., :MLA_NOPE], mla_k_nope_norm)
    k_full = jnp.concatenate(
        [k_nope, jnp.broadcast_to(all_kpe[:, :, None, :], (b, s_len, MLA_HEADS, MLA_ROPE))], axis=-1)
    q_full = jnp.concatenate([q_nope, q_pe], axis=-1)
    o_mla = mla_attention(q_full, k_full, kv[..., MLA_NOPE:], pos, kpos).reshape(b, t, MLA_HEADS * MLA_V)
    o_dsa = dsa_attention(q_d, qi, wi, all_k, all_v, all_ki, pos, kpos).reshape(b, t, DSA_HEADS * DSA_HD)

    g = jax.nn.sigmoid(gates.astype(jnp.float32)).astype(x.dtype)
    g_mla, g_dsa = jnp.split(g, 2, axis=-1)
    x = x + (g_mla * (o_mla @ w_o_mla) + g_dsa * (o_dsa @ w_o_dsa)) @ w_out

    gate_pre, up = jnp.split(rms_norm(x, ffn_norm) @ w_ffn_up, 2, axis=-1)
    padded = jnp.concatenate([conv_hist, gate_pre], axis=1)
    conv = conv_b + sum(conv_w[j] * padded[:, j:j + t] for j in range(CONV_W))
    y = x + (jax.nn.silu(conv) * up) @ w_ffn_down
    new_state = (c_kv, k_pe, k_new, v_new, ki, padded[:, -(CONV_W - 1):])
    return y, new_state


def setup_inputs(seed: int = 0) -> dict:
    key = jax.random.key(seed)
    ks = jax.random.split(key, 28)

    def nrm(k, shape, scale):
        return jax.random.normal(k, shape, jnp.float32) * scale

    def gain(k, n):
        return 1.0 + 0.02 * jax.random.normal(k, (DEPTH, n), jnp.float32)

    return {
        'x_prompt': nrm(ks[0], (BATCH, SEQ, D_MODEL), 1.0),
        'x_sample': nrm(ks[1], (DEC_BATCH, DEC_SEQ, D_MODEL), 1.0),
        'cache_mla_ckv': nrm(ks[2], (DEPTH, DEC_BATCH, PAST_LEN, KV_LORA), 1.0),
        'cache_mla_kpe': nrm(ks[3], (DEPTH, DEC_BATCH, PAST_LEN, MLA_ROPE), 1.0),
        'cache_dsa_k': nrm(ks[4], (DEPTH, DEC_BATCH, PAST_LEN, DSA_KV_HEADS, DSA_HD), 1.0),
        'cache_dsa_v': nrm(ks[5], (DEPTH, DEC_BATCH, PAST_LEN, DSA_KV_HEADS, DSA_HD), 1.0),
        'cache_idx_k': nrm(ks[6], (DEPTH, DEC_BATCH, PAST_LEN, IDX_DIM), 1.0),
        'state_ffn_conv': nrm(ks[7], (DEPTH, DEC_BATCH, CONV_W - 1, D_FF), 1.0),
        'attn_norm': gain(ks[8], D_MODEL),
        'w_in': nrm(ks[9], (DEPTH, D_MODEL, IN_COLS), D_MODEL ** -0.5),
        'q_a_norm': gain(ks[10], Q_LORA),
        'w_q_up': nrm(ks[11], (DEPTH, Q_LORA, MLA_HEADS * (MLA_NOPE + MLA_ROPE)), Q_LORA ** -0.5),
        'kv_a_norm': gain(ks[12], KV_LORA),
        'w_kv_up': nrm(ks[13], (DEPTH, KV_LORA, MLA_HEADS * (MLA_NOPE + MLA_V)), KV_LORA ** -0.5),
        'mla_q_nope_norm': gain(ks[14], MLA_NOPE),
        'mla_q_rope_norm': gain(ks[15], MLA_ROPE),
        'mla_k_nope_norm': gain(ks[16], MLA_NOPE),
        'mla_k_rope_norm': gain(ks[17], MLA_ROPE),
        'dsa_q_norm': gain(ks[18], DSA_HD),
        'dsa_k_norm': gain(ks[19], DSA_HD),
        'w_o_mla': nrm(ks[20], (DEPTH, MLA_HEADS * MLA_V, D_MODEL), (MLA_HEADS * MLA_V) ** -0.5),
        'w_o_dsa': nrm(ks[21], (DEPTH, DSA_HEADS * DSA_HD, D_MODEL), (DSA_HEADS * DSA_HD) ** -0.5),
        'w_out': nrm(ks[22], (DEPTH, D_MODEL, D_MODEL), D_MODEL ** -0.5),
        'ffn_norm': gain(ks[23], D_MODEL),
        'w_ffn_up': nrm(ks[24], (DEPTH, D_MODEL, 2 * D_FF), D_MODEL ** -0.5),
        'conv_w': nrm(ks[25], (DEPTH, CONV_W, D_FF), CONV_W ** -0.5),
        'conv_b': nrm(ks[26], (DEPTH, D_FF), 0.02),
        'w_ffn_down': nrm(ks[27], (DEPTH, D_FF, D_MODEL), D_FF ** -0.5),
    }


def reference(x_prompt, x_sample, cache_mla_ckv, cache_mla_kpe, cache_dsa_k, cache_dsa_v,
              cache_idx_k, state_ffn_conv, attn_norm, w_in, q_a_norm, w_q_up, kv_a_norm, w_kv_up,
              mla_q_nope_norm, mla_q_rope_norm, mla_k_nope_norm, mla_k_rope_norm, dsa_q_norm,
              dsa_k_norm, w_o_mla, w_o_dsa, w_out, ffn_norm, w_ffn_up, conv_w, conv_b, w_ffn_down):
    pos_p = jnp.arange(x_prompt.shape[1], dtype=jnp.int32)
    pos_s = cache_mla_ckv.shape[2] + jnp.arange(x_sample.shape[1], dtype=jnp.int32)
    xp, xs = x_prompt, x_sample
    p_states, s_states = [], []
    for l in range(DEPTH):
        prm = (attn_norm[l], w_in[l], q_a_norm[l], w_q_up[l], kv_a_norm[l], w_kv_up[l],
               mla_q_nope_norm[l], mla_q_rope_norm[l], mla_k_nope_norm[l], mla_k_rope_norm[l],
               dsa_q_norm[l], dsa_k_norm[l], w_o_mla[l], w_o_dsa[l], w_out[l], ffn_norm[l],
               w_ffn_up[l], conv_w[l], conv_b[l], w_ffn_down[l])
        past = (cache_mla_ckv[l], cache_mla_kpe[l], cache_dsa_k[l], cache_dsa_v[l],
                cache_idx_k[l], state_ffn_conv[l])
        xp, sp = _layer(xp, pos_p, None, prm)
        xs, ss = _layer(xs, pos_s, past, prm)
        p_states.append(sp)
        s_states.append(ss)
    p_mla_ckv, p_mla_kpe, p_dsa_k, p_dsa_v, p_idx_k, p_ffn_conv = [jnp.stack(a) for a in zip(*p_states)]
    s_mla_ckv, s_mla_kpe, s_dsa_k, s_dsa_v, s_idx_k, s_ffn_conv = [jnp.stack(a) for a in zip(*s_states)]
    return (xp, xs, p_mla_ckv, p_mla_kpe, p_dsa_k, p_dsa_v, p_idx_k, p_ffn_conv,
            s_mla_ckv, s_mla_kpe, s_dsa_k, s_dsa_v, s_idx_k, s_ffn_conv)
```

```python
import functools

import jax
import jax.numpy as jnp
from jax import lax
from jax.experimental import pallas as pl
from jax.experimental.pallas import tpu as pltpu

CHUNK = 64
CHUNK_SHIFT = 6
ROPE_THETA = 500000.0
NORM_EPS = 1e-6
NEG_INF = -1e30
MLA_HEADS = 8
Q_LORA = 512
KV_LORA = 256
MLA_NOPE = 128
MLA_ROPE = 64
MLA_V = 128
MLA_QK_PAD = 256
DSA_HEADS = 8
DSA_KV_HEADS = 2
DSA_HD = 128
DSA_ROT = DSA_HD // 4
IDX_HEADS = 16
IDX_DIM = 64
IDX_ROT = IDX_DIM // 4
TOPK_MAX = 256
D_FF = 5632
CONV_W = 3

LANES = 128
VMEM_LIMIT = 56 * 1024 * 1024
INT_MIN = -2 ** 31

F32 = jnp.float32
BF16 = jnp.bfloat16


def _cparams(sem):
    return pltpu.CompilerParams(dimension_semantics=sem, vmem_limit_bytes=VMEM_LIMIT)


def _resident(shape):
    nd = len(shape)
    return pl.BlockSpec(shape, lambda *_: (0,) * nd, pipeline_mode=pl.Buffered(1))


def _rms(x, g, n=None):
    n = x.shape[-1] if n is None else n
    ms = jnp.sum(x * x, axis=-1, keepdims=True) * (1.0 / n)
    return x * lax.rsqrt(ms + NORM_EPS) * g


def _rope128(x, c, s, half):
    lane = lax.broadcasted_iota(jnp.int32, x.shape, 1)
    swapped = jnp.where(lane < half, pltpu.roll(x, LANES - half, 1), pltpu.roll(x, half, 1))
    return x * c + swapped * s


def _dot(a, b):
    return jnp.dot(a, b, preferred_element_type=F32)


def _dot_t(a, b):
    return lax.dot_general(a, b, (((1,), (1,)), ((), ())), preferred_element_type=F32)


def _in_proj_kernel(x_ref, gattn_ref, wcq_ref, wsm_ref, wqd_ref, wkvd_ref, wqi_ref, wqup_ref,
                    gqa_ref, gkva_ref, gkpe_ref, gqn_ref, gqr_ref, gdq_ref, gdk_ref,
                    c64_ref, s64_ref, c32_ref, s32_ref, c16_ref, s16_ref,
                    h_ref, qfull_ref, ckv_ref, kpe_ref, kpeb_ref, qd_ref, knew_ref, vnew_ref,
                    knewb_ref, vnewb_ref, qi_ref, ki_ref, kib_ref, wi_ref):
    x = x_ref[...]
    hb = _rms(x, gattn_ref[...]).astype(BF16)
    h_ref[...] = hb
    c64, s64 = c64_ref[...], s64_ref[...]
    c32, s32 = c32_ref[...], s32_ref[...]
    c16, s16 = c16_ref[...], s16_ref[...]

    cqn = _rms(_dot(hb, wcq_ref[...]), gqa_ref[...]).astype(BF16)
    q = _dot(cqn, wqup_ref[...])
    scale = (MLA_NOPE + MLA_ROPE) ** -0.5
    for hd in range(MLA_HEADS):
        lo = hd * MLA_QK_PAD
        nope = _rms(q[:, lo:lo + LANES], gqn_ref[...]) * scale
        qfull_ref[:, lo:lo + LANES] = nope.astype(BF16)
        pe = _rms(q[:, lo + LANES:lo + 2 * LANES], gqr_ref[...], MLA_ROPE)
        pe = _rope128(pe, c64, s64, MLA_ROPE // 2) * scale
        qfull_ref[:, lo + LANES:lo + 2 * LANES] = pe.astype(BF16)

    sm = _dot(hb, wsm_ref[...])
    ckv_ref[...] = _rms(sm[:, :KV_LORA], gkva_ref[...])
    kpe = _rope128(_rms(sm[:, KV_LORA:KV_LORA + LANES], gkpe_ref[...], MLA_ROPE), c64, s64,
                   MLA_ROPE // 2)
    kpe_ref[...] = kpe[:, :MLA_ROPE]
    kpeb_ref[...] = kpe.astype(BF16)
    ki = _rope128(sm[:, KV_LORA + LANES:KV_LORA + 2 * LANES], c16, s16, IDX_ROT // 2)
    ki_ref[...] = ki[:, :IDX_DIM]
    kib_ref[...] = ki.astype(BF16)
    wi_ref[...] = sm[:, KV_LORA + 2 * LANES:KV_LORA + 2 * LANES + IDX_HEADS] * (IDX_HEADS ** -0.5)

    qd = _dot(hb, wqd_ref[...])
    for hd in range(DSA_HEADS):
        lo = hd * DSA_HD
        t = _rope128(_rms(qd[:, lo:lo + DSA_HD], gdq_ref[...]), c32, s32, DSA_ROT // 2)
        qd_ref[:, lo:lo + DSA_HD] = (t * (DSA_HD ** -0.5)).astype(BF16)
    kvd = _dot(hb, wkvd_ref[...])
    for hd in range(DSA_KV_HEADS):
        lo = hd * DSA_HD
        t = _rope128(_rms(kvd[:, lo:lo + DSA_HD], gdk_ref[...]), c32, s32, DSA_ROT // 2)
        knew_ref[:, lo:lo + DSA_HD] = t
        knewb_ref[:, lo:lo + DSA_HD] = t.astype(BF16)
    v = kvd[:, DSA_KV_HEADS * DSA_HD:]
    vnew_ref[...] = v
    vnewb_ref[...] = v.astype(BF16)

    qi = _dot(hb, wqi_ref[...])
    for hd in range(IDX_HEADS):
        lo = hd * LANES
        t = _rope128(qi[:, lo:lo + LANES], c16, s16, IDX_ROT // 2)
        qi_ref[:, lo:lo + LANES] = (t * (IDX_DIM ** -0.5)).astype(BF16)


def _in_proj(xf, tabs, w, tm, t_len):
    n, d = xf.shape
    n_tiles = n // tm
    if t_len >= tm:
        per = t_len // tm
        tab_map = lambda i: (i % per, 0)
    else:
        tab_map = lambda i: (0, 0)
    row = lambda width: pl.BlockSpec((tm, width), lambda i: (i, 0))
    tab = pl.BlockSpec((tm, LANES), tab_map)
    weights = [w['attn_norm'], w['w_cq'], w['w_sm'], w['w_qd'], w['w_kvd'], w['w_qi'], w['w_qup'],
               w['q_a_norm'], w['kv_a_norm'], w['g_kpe'], w['g_qnope'], w['g_qrope'],
               w['dsa_q_norm'], w['dsa_k_norm']]
    out_widths = [(d, BF16), (MLA_HEADS * MLA_QK_PAD, BF16), (KV_LORA, F32), (MLA_ROPE, F32),
                  (LANES, BF16), (DSA_HEADS * DSA_HD, BF16), (DSA_KV_HEADS * DSA_HD, F32),
                  (DSA_KV_HEADS * DSA_HD, F32), (DSA_KV_HEADS * DSA_HD, BF16),
                  (DSA_KV_HEADS * DSA_HD, BF16), (IDX_HEADS * LANES, BF16), (IDX_DIM, F32),
                  (LANES, BF16), (IDX_HEADS, F32)]
    return pl.pallas_call(
        _in_proj_kernel,
        grid=(n_tiles,),
        in_specs=[row(d)] + [_resident(a.shape) for a in weights] + [tab] * 6,
        out_specs=[row(wd) for wd, _ in out_widths],
        out_shape=[jax.ShapeDtypeStruct((n, wd), dt) for wd, dt in out_widths],
        compiler_params=_cparams(("parallel",)),
        name="in_proj",
    )(xf, *weights, *tabs)


def _kv_up_kernel(ckv_ref, kpe_ref, w_ref, g_ref, kfull_ref, v_ref):
    kv = _dot(ckv_ref[...].astype(BF16), w_ref[...])
    kpe = kpe_ref[...]
    g = g_ref[...]
    for hd in range(MLA_HEADS):
        lo = hd * MLA_QK_PAD
        kfull_ref[:, lo:lo + LANES] = _rms(kv[:, hd * LANES:(hd + 1) * LANES], g).astype(BF16)
        kfull_ref[:, lo + LANES:lo + 2 * LANES] = kpe
    v_ref[...] = kv[:, MLA_HEADS * MLA_NOPE:].astype(BF16)


def _kv_up(ckv, kpeb, w, tm):
    n = ckv.shape[0]
    row = lambda width: pl.BlockSpec((tm, width), lambda i: (i, 0))
    return pl.pallas_call(
        _kv_up_kernel,
        grid=(n // tm,),
        in_specs=[row(KV_LORA), row(LANES), _resident(w['w_kvup'].shape),
                  _resident(w['g_knope'].shape)],
        out_specs=[row(MLA_HEADS * MLA_QK_PAD), row(MLA_HEADS * MLA_V)],
        out_shape=[jax.ShapeDtypeStruct((n, MLA_HEADS * MLA_QK_PAD), BF16),
                   jax.ShapeDtypeStruct((n, MLA_HEADS * MLA_V), BF16)],
        compiler_params=_cparams(("parallel",)),
        name="kv_up",
    )(ckv, kpeb, w['w_kvup'], w['g_knope'])


def _visible_limit(q_last_pos, s_valid):
    return jnp.minimum(((q_last_pos >> CHUNK_SHIFT) + 1) * CHUNK, s_valid)


def _mla_kernel(q_ref, k_ref, v_ref, o_ref, *, tq, tk, pos0, s_valid):
    qt = pl.program_id(2)
    q = q_ref[...]
    q_first = pos0 + qt * tq
    qchunk = (q_first + lax.broadcasted_iota(jnp.int32, (tq, 1), 0)) >> CHUNK_SHIFT
    n_chunks = pl.cdiv(_visible_limit(q_first + tq - 1, s_valid), tk)

    def body(c, carry):
        m, l, acc = carry
        start = pl.multiple_of(c * tk, tk)
        s = _dot_t(q, k_ref[pl.ds(start, tk), :])
        kpos = start + lax.broadcasted_iota(jnp.int32, (1, tk), 1)
        s = jnp.where(((kpos >> CHUNK_SHIFT) <= qchunk) & (kpos < s_valid), s, NEG_INF)
        m_new = jnp.maximum(m, jnp.max(s, axis=-1, keepdims=True))
        alpha = jnp.exp(m - m_new)
        p = jnp.exp(s - m_new)
        l = alpha * l + jnp.sum(p, axis=-1, keepdims=True)
        acc = alpha * acc + _dot(p.astype(BF16), v_ref[pl.ds(start, tk), :])
        return m_new, l, acc

    init = (jnp.full((tq, 1), NEG_INF, F32), jnp.zeros((tq, 1), F32), jnp.zeros((tq, MLA_V), F32))
    _, l, acc = lax.fori_loop(0, n_chunks, body, init)
    o_ref[...] = (acc / l).astype(BF16)


def _mla_attn(qfull, kfull, vfull, b, t_len, s_pad, tq, tk, pos0, s_valid):
    nq = t_len // tq
    kern = functools.partial(_mla_kernel, tq=tq, tk=tk, pos0=pos0, s_valid=s_valid)
    return pl.pallas_call(
        kern,
        grid=(b, MLA_HEADS, nq),
        in_specs=[pl.BlockSpec((tq, MLA_QK_PAD), lambda bi, h, i: (bi * nq + i, h)),
                  pl.BlockSpec((s_pad, MLA_QK_PAD), lambda bi, h, i: (bi, h)),
                  pl.BlockSpec((s_pad, MLA_V), lambda bi, h, i: (bi, h))],
        out_specs=pl.BlockSpec((tq, MLA_V), lambda bi, h, i: (bi * nq + i, h)),
        out_shape=jax.ShapeDtypeStruct((b * t_len, MLA_HEADS * MLA_V), BF16),
        compiler_params=_cparams(("parallel", "parallel", "arbitrary")),
        name="mla_attn",
    )(qfull, kfull, vfull)


def _count_lanes(pred_fn, key_scr, n_chunks, tq, tk):
    def body(c, part):
        keys = key_scr[c]
        for j in range(tk // LANES):
            idx = c * tk + j * LANES + lax.broadcasted_iota(jnp.int32, (1, LANES), 1)
            part = part + jnp.where(pred_fn(keys[:, j * LANES:(j + 1) * LANES], idx), 1, 0)
        return part
    part = lax.fori_loop(0, n_chunks, body, jnp.zeros((tq, LANES), jnp.int32))
    return jnp.sum(part, axis=-1, keepdims=True)


def _dsa_kernel(qi_ref, wi_ref, qd_ref, ki_ref, k_ref, v_ref, o_ref, key_scr, bias_scr, cut_scr,
                *, tq, tk, pos0, s_valid, topk, idx_bits):
    qt = pl.program_id(1)
    q_first = pos0 + qt * tq
    qchunk = (q_first + lax.broadcasted_iota(jnp.int32, (tq, 1), 0)) >> CHUNK_SHIFT
    n_chunks = pl.cdiv(_visible_limit(q_first + tq - 1, s_valid), tk)
    wi = wi_ref[...]

    def score_body(c, _):
        start = pl.multiple_of(c * tk, tk)
        kic = ki_ref[pl.ds(start, tk), :]
        sc = jnp.zeros((tq, tk), F32)
        for hd in range(IDX_HEADS):
            d = _dot_t(qi_ref[:, hd * LANES:(hd + 1) * LANES], kic)
            sc = sc + wi[:, hd:hd + 1] * jnp.maximum(d, 0.0)
        bits = pltpu.bitcast(sc + 0.0, jnp.int32)
        keys = bits ^ ((bits >> 31) & 0x7FFFFFFF)
        kpos = start + lax.broadcasted_iota(jnp.int32, (1, tk), 1)
        vis = ((kpos >> CHUNK_SHIFT) <= qchunk) & (kpos < s_valid)
        key_scr[c] = jnp.where(vis, keys, INT_MIN)
        return 0
    lax.fori_loop(0, n_chunks, score_body, 0)

    def bit_body(b, t):
        cand = t + (jnp.int32(1) << (31 - b))
        cnt = _count_lanes(lambda kk, _: kk >= cand, key_scr, n_chunks, tq, tk)
        return jnp.where(cnt >= topk, cand, t)
    thr = lax.fori_loop(0, 32, bit_body, jnp.full((tq, 1), INT_MIN, jnp.int32))

    cnt_gt = _count_lanes(lambda kk, _: kk > thr, key_scr, n_chunks, tq, tk)
    cnt_eq = _count_lanes(lambda kk, _: kk == thr, key_scr, n_chunks, tq, tk)
    need = topk - cnt_gt
    cut_scr[...] = jnp.full((tq, LANES), 2 ** 30, jnp.int32)
    excess = jnp.max(jnp.where((cnt_eq > need) & (thr > INT_MIN), 1, 0))

    @pl.when(excess > 0)
    def _():
        def idx_body(b, pos):
            cand = pos + (jnp.int32(1) << (idx_bits - 1 - b))
            cnt = _count_lanes(lambda kk, idx: (kk == thr) & (idx < cand), key_scr, n_chunks, tq, tk)
            return jnp.where(cnt < need, cand, pos)
        pos = lax.fori_loop(0, idx_bits, idx_body, jnp.zeros((tq, 1), jnp.int32))
        cut_scr[...] = jnp.broadcast_to(pos, (tq, LANES))
    cut = cut_scr[:, 0:1]

    def bias_body(c, _):
        keys = key_scr[c]
        idx = c * tk + lax.broadcasted_iota(jnp.int32, (1, tk), 1)
        sel = (keys > thr) | ((keys == thr) & (idx <= cut))
        sel = sel & (keys > INT_MIN)
        bias_scr[c] = jnp.where(sel, 0.0, NEG_INF)
        return 0
    lax.fori_loop(0, n_chunks, bias_body, 0)

    rep = DSA_HEADS // DSA_KV_HEADS
    for hd in range(DSA_HEADS):
        g = hd // rep
        q = qd_ref[:, hd * DSA_HD:(hd + 1) * DSA_HD]

        def att_body(c, carry, q=q, g=g):
            m, l, acc = carry
            start = pl.multiple_of(c * tk, tk)
            s = _dot_t(q, k_ref[pl.ds(start, tk), g * DSA_HD:(g + 1) * DSA_HD]) + bias_scr[c]
            m_new = jnp.maximum(m, jnp.max(s, axis=-1, keepdims=True))
            alpha = jnp.exp(m - m_new)
            p = jnp.exp(s - m_new)
            l = alpha * l + jnp.sum(p, axis=-1, keepdims=True)
            acc = alpha * acc + _dot(p.astype(BF16),
                                     v_ref[pl.ds(start, tk), g * DSA_HD:(g + 1) * DSA_HD])
            return m_new, l, acc

        init = (jnp.full((tq, 1), NEG_INF, F32), jnp.zeros((tq, 1), F32),
                jnp.zeros((tq, DSA_HD), F32))
        _, l, acc = lax.fori_loop(0, n_chunks, att_body, init)
        o_ref[:, hd * DSA_HD:(hd + 1) * DSA_HD] = (acc / l).astype(BF16)


def _dsa_attn(qi, wi, qd, kib, kb, vb, b, t_len, s_pad, tq, tk, pos0, s_valid, topk):
    nq = t_len // tq
    n_ch = s_pad // tk
    idx_bits = max(1, (s_pad - 1).bit_length())
    kern = functools.partial(_dsa_kernel, tq=tq, tk=tk, pos0=pos0, s_valid=s_valid, topk=topk,
                             idx_bits=idx_bits)
    qrow = lambda width: pl.BlockSpec((tq, width), lambda bi, i: (bi * nq + i, 0))
    krow = lambda width: pl.BlockSpec((s_pad, width), lambda bi, i: (bi, 0))
    return pl.pallas_call(
        kern,
        grid=(b, nq),
        in_specs=[qrow(IDX_HEADS * LANES), qrow(IDX_HEADS), qrow(DSA_HEADS * DSA_HD),
                  krow(LANES), krow(DSA_KV_HEADS * DSA_HD), krow(DSA_KV_HEADS * DSA_HD)],
        out_specs=qrow(DSA_HEADS * DSA_HD),
        out_shape=jax.ShapeDtypeStruct((b * t_len, DSA_HEADS * DSA_HD), BF16),
        scratch_shapes=[pltpu.VMEM((n_ch, tq, tk), jnp.int32), pltpu.VMEM((n_ch, tq, tk), F32),
                        pltpu.VMEM((tq, LANES), jnp.int32)],
        compiler_params=_cparams(("parallel", "arbitrary")),
        name="dsa_attn",
    )(qi, wi, qd, kib, kb, vb)


def _sigmoid(x):
    return 1.0 / (1.0 + jnp.exp(-x))


def _merge_out_kernel(h_ref, om_ref, od_ref, x_ref, wgm_ref, wgd_ref, wom_ref, wod_ref, wout_ref,
                      gffn_ref, x1_ref, hf_ref, acc_scr):
    j = pl.program_id(1)

    @pl.when(j == 0)
    def _():
        acc_scr[...] = x_ref[...]

    h = h_ref[...]
    merged = (_sigmoid(_dot(h, wgm_ref[...])) * _dot(om_ref[...], wom_ref[...])
              + _sigmoid(_dot(h, wgd_ref[...])) * _dot(od_ref[...], wod_ref[...]))
    acc_scr[...] += _dot(merged.astype(BF16), wout_ref[...])

    @pl.when(j == pl.num_programs(1) - 1)
    def _():
        x1 = acc_scr[...]
        x1_ref[...] = x1
        hf_ref[...] = _rms(x1, gffn_ref[...]).astype(BF16)


def _merge_out(h, om, od, xf, w, tm, tn):
    n, d = xf.shape
    row = lambda width: pl.BlockSpec((tm, width), lambda i, j: (i, 0))
    col = lambda rows: pl.BlockSpec((rows, tn), lambda i, j: (0, j))
    return pl.pallas_call(
        _merge_out_kernel,
        grid=(n // tm, d // tn),
        in_specs=[row(d), row(om.shape[1]), row(od.shape[1]), row(d),
                  col(d), col(d), col(om.shape[1]), col(od.shape[1]),
                  pl.BlockSpec((tn, d), lambda i, j: (j, 0)),
                  pl.BlockSpec((1, d), lambda i, j: (0, 0))],
        out_specs=[row(d), row(d)],
        out_shape=[jax.ShapeDtypeStruct((n, d), F32), jax.ShapeDtypeStruct((n, d), BF16)],
        scratch_shapes=[pltpu.VMEM((tm, d), F32)],
        compiler_params=_cparams(("parallel", "arbitrary")),
        name="merge_out",
    )(h, om, od, xf, w['w_gm'], w['w_gd'], w['w_om'], w['w_od'], w['w_out'], w['ffn_norm'])


def _gate_rows_kernel(h_ref, w_ref, o_ref):
    o_ref[...] = _dot(h_ref[...], w_ref[...])


def _gate_rows(rows, w_gate, tf):
    r, d = rows.shape
    return pl.pallas_call(
        _gate_rows_kernel,
        grid=(D_FF // tf,),
        in_specs=[pl.BlockSpec((r, d), lambda j: (0, 0)), pl.BlockSpec((d, tf), lambda j: (0, j))],
        out_specs=pl.BlockSpec((r, tf), lambda j: (0, j)),
        out_shape=jax.ShapeDtypeStruct((r, D_FF), F32),
        compiler_params=_cparams(("parallel",)),
        name="ffn_gate_rows",
    )(rows, w_gate)


def _split3(x):
    hi = x.astype(BF16)
    r1 = x - hi.astype(F32)
    mid = r1.astype(BF16)
    lo = (r1 - mid.astype(F32)).astype(BF16)
    return hi, mid, lo


def _ffn_kernel(hf_ref, x1_ref, halo_ref, wg_ref, wu_ref, wd_ref, cw_ref, cb_ref, y_ref, acc_scr,
                *, tm, t_seg, halo_rows):
    j = pl.program_id(1)

    @pl.when(j == 0)
    def _():
        acc_scr[...] = x1_ref[...]

    hf = hf_ref[...]
    g = _dot(hf, wg_ref[...])
    up = _dot(hf, wu_ref[...])

    seg_shift = t_seg.bit_length() - 1
    row = lax.broadcasted_iota(jnp.int32, (tm, halo_rows), 0)
    colh = lax.broadcasted_iota(jnp.int32, (tm, halo_rows), 1)
    seg2 = (row >> seg_shift) * 2
    r_in = row & (t_seg - 1)
    e1 = jnp.where((r_in == 0) & (colh == seg2 + 1), 1.0, 0.0).astype(BF16)
    e2 = jnp.where(((r_in == 0) & (colh == seg2)) | ((r_in == 1) & (colh == seg2 + 1)),
                   1.0, 0.0).astype(BF16)
    pieces = _split3(halo_ref[0])
    fix1 = _dot(e1, pieces[0]) + _dot(e1, pieces[1]) + _dot(e1, pieces[2])
    fix2 = _dot(e2, pieces[0]) + _dot(e2, pieces[1]) + _dot(e2, pieces[2])
    r1 = lax.broadcasted_iota(jnp.int32, (tm, 1), 0) & (t_seg - 1)
    g1 = jnp.where(r1 < 1, fix1, pltpu.roll(g, 1, 0))
    g2 = jnp.where(r1 < 2, fix2, pltpu.roll(g, 2, 0))
    cw = cw_ref[...]
    conv = cb_ref[...] + cw[0:1, :] * g2 + cw[1:2, :] * g1 + cw[2:3, :] * g
    act = conv * _sigmoid(conv) * up
    acc_scr[...] += _dot(act.astype(BF16), wd_ref[...])

    @pl.when(j == pl.num_programs(1) - 1)
    def _():
        y_ref[...] = acc_scr[...]


def _ffn(hf, x1, halo, w, tm, tf, t_seg):
    n, d = x1.shape
    halo_rows = halo.shape[1]
    kern = functools.partial(_ffn_kernel, tm=tm, t_seg=t_seg, halo_rows=halo_rows)
    row = lambda width: pl.BlockSpec((tm, width), lambda i, j: (i, 0))
    return pl.pallas_call(
        kern,
        grid=(n // tm, D_FF // tf),
        in_specs=[row(d), row(d),
                  pl.BlockSpec((1, halo_rows, tf), lambda i, j: (i, 0, j)),
                  pl.BlockSpec((d, tf), lambda i, j: (0, j)),
                  pl.BlockSpec((d, tf), lambda i, j: (0, j)),
                  pl.BlockSpec((tf, d), lambda i, j: (j, 0)),
                  pl.BlockSpec((CONV_W, tf), lambda i, j: (0, j)),
                  pl.BlockSpec((1, tf), lambda i, j: (0, j))],
        out_specs=row(d),
        out_shape=jax.ShapeDtypeStruct((n, d), F32),
        scratch_shapes=[pltpu.VMEM((tm, d), F32)],
        compiler_params=_cparams(("parallel", "arbitrary")),
        name="conv_ffn",
    )(hf, x1, halo, w['w_ffn_gate'], w['w_ffn_upv'], w['w_ffn_down'], w['conv_w'], w['conv_b'])


def _pad_cols(a, width):
    return jnp.pad(a, ((0, 0), (0, width - a.shape[1])))


def _prep_weights(attn_norm, w_in, q_a_norm, w_q_up, kv_a_norm, w_kv_up, mla_q_nope_norm,
                  mla_q_rope_norm, mla_k_nope_norm, mla_k_rope_norm, dsa_q_norm, dsa_k_norm,
                  w_o_mla, w_o_dsa, w_out, ffn_norm, w_ffn_up, conv_w, conv_b, w_ffn_down):
    d = w_in.shape[0]
    widths = (Q_LORA, KV_LORA + MLA_ROPE, DSA_HEADS * DSA_HD, 2 * DSA_KV_HEADS * DSA_HD,
              IDX_HEADS * IDX_DIM, IDX_DIM, IDX_HEADS, 2 * d)
    offs = [0]
    for wd in widths:
        offs.append(offs[-1] + wd)
    seg = [w_in[:, offs[i]:offs[i + 1]] for i in range(len(widths))]
    w = {}
    w['w_cq'] = seg[0].astype(BF16)
    w['w_sm'] = jnp.concatenate(
        [seg[1][:, :KV_LORA], _pad_cols(seg[1][:, KV_LORA:], LANES), _pad_cols(seg[5], LANES),
         _pad_cols(seg[6], LANES)], axis=1).astype(BF16)
    w['w_qd'] = seg[2].astype(BF16)
    w['w_kvd'] = seg[3].astype(BF16)
    w['w_qi'] = jnp.pad(seg[4].reshape(d, IDX_HEADS, IDX_DIM),
                        ((0, 0), (0, 0), (0, LANES - IDX_DIM))).reshape(d, IDX_HEADS * LANES).astype(BF16)
    w['w_gm'] = seg[7][:, :d].astype(BF16)
    w['w_gd'] = seg[7][:, d:].astype(BF16)
    qup = w_q_up.reshape(Q_LORA, MLA_HEADS, MLA_NOPE + MLA_ROPE)
    w['w_qup'] = jnp.pad(qup, ((0, 0), (0, 0), (0, MLA_QK_PAD - MLA_NOPE - MLA_ROPE))
                         ).reshape(Q_LORA, MLA_HEADS * MLA_QK_PAD).astype(BF16)
    kvup = w_kv_up.reshape(KV_LORA, MLA_HEADS, MLA_NOPE + MLA_V)
    w['w_kvup'] = jnp.concatenate(
        [kvup[:, :, :MLA_NOPE].reshape(KV_LORA, MLA_HEADS * MLA_NOPE),
         kvup[:, :, MLA_NOPE:].reshape(KV_LORA, MLA_HEADS * MLA_V)], axis=1).astype(BF16)
    r2 = lambda a: a.reshape(1, -1).astype(F32)
    w['attn_norm'] = r2(attn_norm)
    w['q_a_norm'] = r2(q_a_norm)
    w['kv_a_norm'] = r2(kv_a_norm)
    w['g_kpe'] = _pad_cols(r2(mla_k_rope_norm), LANES)
    w['g_qnope'] = r2(mla_q_nope_norm)
    w['g_qrope'] = _pad_cols(r2(mla_q_rope_norm), LANES)
    w['g_knope'] = r2(mla_k_nope_norm)
    w['dsa_q_norm'] = r2(dsa_q_norm)
    w['dsa_k_norm'] = r2(dsa_k_norm)
    w['ffn_norm'] = r2(ffn_norm)
    w['w_om'] = w_o_mla.astype(BF16)
    w['w_od'] = w_o_dsa.astype(BF16)
    w['w_out'] = w_out.astype(BF16)
    w['w_ffn_gate'] = w_ffn_up[:, :D_FF].astype(BF16)
    w['w_ffn_upv'] = w_ffn_up[:, D_FF:].astype(BF16)
    w['w_ffn_down'] = w_ffn_down.astype(BF16)
    w['conv_w'] = conv_w.astype(F32)
    w['conv_b'] = r2(conv_b)
    return w


def _rope_tables(pos, rot):
    half = rot // 2
    inv = ROPE_THETA ** (-jnp.arange(half, dtype=F32) / half)
    ang = pos.astype(F32)[:, None] * inv[None, :]
    cos, sin = jnp.cos(ang), jnp.sin(ang)
    t = pos.shape[0]
    c = jnp.concatenate([cos, cos, jnp.ones((t, LANES - rot), F32)], axis=1)
    s = jnp.concatenate([-sin, sin, jnp.zeros((t, LANES - rot), F32)], axis=1)
    return c, s


def _round_up(a, m):
    return (a + m - 1) // m * m


def _tile(n, target):
    t = min(n, target)
    assert n % t == 0, (n, t)
    return t


def _layer(x, pos0, past, w):
    b, t_len, d = x.shape
    n = b * t_len
    xf = x.reshape(n, d)

    tm = _tile(n, 512)
    tm_in = _tile(n, 256)
    assert tm % t_len == 0 or t_len % tm == 0
    assert tm_in % t_len == 0 or t_len % tm_in == 0
    pos = pos0 + jnp.arange(t_len, dtype=jnp.int32)
    tabs = []
    for rot in (MLA_ROPE, DSA_ROT, IDX_ROT):
        c, s = _rope_tables(pos, rot)
        if t_len < tm_in:
            c, s = jnp.tile(c, (tm_in // t_len, 1)), jnp.tile(s, (tm_in // t_len, 1))
        tabs += [c, s]

    (h, qfull, ckv, kpe, kpeb, qd, knew, vnew, knewb, vnewb, qi, ki, kib, wi) = _in_proj(
        xf, tabs, w, tm_in, t_len)

    if past is None:
        s_len = t_len
        tk = _tile(s_len, 512)
        s_pad = s_len
        all_ckv, all_kpeb, all_kb, all_vb, all_kib = ckv, kpeb, knewb, vnewb, kib
        conv_hist = jnp.zeros((b, CONV_W - 1, D_FF), F32)
    else:
        p_ckv, p_kpe, p_k, p_v, p_ki, conv_hist = past
        past_len = p_ckv.shape[1]
        s_len = past_len + t_len
        tk = 384 if s_len > 384 else _round_up(s_len, LANES)
        s_pad = _round_up(s_len, tk)
        padr = s_pad - s_len

        def cat(p_arr, new, width, dtype, lane_pad=0):
            p2 = p_arr.reshape(b, past_len, width).astype(dtype)
            if lane_pad:
                p2 = jnp.pad(p2, ((0, 0), (0, 0), (0, lane_pad)))
            a = jnp.concatenate([p2, new.reshape(b, t_len, width + lane_pad)], axis=1)
            return jnp.pad(a, ((0, 0), (0, padr), (0, 0))).reshape(b * s_pad, width + lane_pad)

        all_ckv = cat(p_ckv, ckv, KV_LORA, F32)
        all_kpeb = cat(p_kpe, kpeb, MLA_ROPE, BF16, LANES - MLA_ROPE)
        all_kb = cat(p_k, knewb, DSA_KV_HEADS * DSA_HD, BF16)
        all_vb = cat(p_v, vnewb, DSA_KV_HEADS * DSA_HD, BF16)
        all_kib = cat(p_ki, kib, IDX_DIM, BF16, LANES - IDX_DIM)

    kfull, vfull = _kv_up(all_ckv, all_kpeb, w, _tile(b * s_pad, 512 if past is None else tk))
    tq = _tile(t_len, 256)
    o_mla = _mla_attn(qfull, kfull, vfull, b, t_len, s_pad, tq, tk, pos0, s_len)
    topk = min(TOPK_MAX, s_len // 4)
    o_dsa = _dsa_attn(qi, wi, qd, all_kib, all_kb, all_vb, b, t_len, s_pad, tq, tk, pos0, s_len,
                      topk)

    x1, hf = _merge_out(h, o_mla, o_dsa, xf, w, tm, 512)

    t_seg = min(t_len, tm)
    assert t_seg & (t_seg - 1) == 0 and t_seg >= CONV_W - 1
    n_t = t_len // t_seg
    segs = tm // t_seg
    brows = hf.reshape(b, n_t, t_seg, d)[:, :, t_seg - (CONV_W - 1):, :].reshape(-1, d)
    bg = _gate_rows(brows, w['w_ffn_gate'], 512).reshape(b, n_t, CONV_W - 1, D_FF)
    halo = jnp.concatenate([conv_hist.astype(F32)[:, None], bg[:, :-1]], axis=1)
    halo = halo.reshape(n // tm, segs * (CONV_W - 1), D_FF)
    halo_rows = _round_up(halo.shape[1], 16)
    halo = jnp.pad(halo, ((0, 0), (0, halo_rows - halo.shape[1]), (0, 0)))
    y = _ffn(hf, x1, halo, w, tm, 512, t_seg)

    state = (ckv.reshape(b, t_len, KV_LORA), kpe.reshape(b, t_len, MLA_ROPE),
             knew.reshape(b, t_len, DSA_KV_HEADS, DSA_HD), vnew.reshape(b, t_len, DSA_KV_HEADS, DSA_HD),
             ki.reshape(b, t_len, IDX_DIM), bg[:, -1])
    return y.reshape(b, t_len, d), state


def kernel(x_prompt, x_sample, cache_mla_ckv, cache_mla_kpe, cache_dsa_k, cache_dsa_v, cache_idx_k, state_ffn_conv, attn_norm, w_in, q_a_norm, w_q_up, kv_a_norm, w_kv_up, mla_q_nope_norm, mla_q_rope_norm, mla_k_nope_norm, mla_k_rope_norm, dsa_q_norm, dsa_k_norm, w_o_mla, w_o_dsa, w_out, ffn_norm, w_ffn_up, conv_w, conv_b, w_ffn_down):
    depth = w_in.shape[0]
    past_len = cache_mla_ckv.shape[2]
    xp, xs = x_prompt, x_sample
    p_states, s_states = [], []
    for l in range(depth):
        w = _prep_weights(attn_norm[l], w_in[l], q_a_norm[l], w_q_up[l], kv_a_norm[l], w_kv_up[l],
                          mla_q_nope_norm[l], mla_q_rope_norm[l], mla_k_nope_norm[l],
                          mla_k_rope_norm[l], dsa_q_norm[l], dsa_k_norm[l], w_o_mla[l], w_o_dsa[l],
                          w_out[l], ffn_norm[l], w_ffn_up[l], conv_w[l], conv_b[l], w_ffn_down[l])
        past = (cache_mla_ckv[l], cache_mla_kpe[l], cache_dsa_k[l], cache_dsa_v[l],
                cache_idx_k[l], state_ffn_conv[l])
        xp, sp = _layer(xp, 0, None, w)
        xs, ss = _layer(xs, past_len, past, w)
        p_states.append(sp)
        s_states.append(ss)
    p_out = [jnp.stack(a) for a in zip(*p_states)]
    s_out = [jnp.stack(a) for a in zip(*s_states)]
    return (xp, xs, *p_out, *s_out)
```

```python
import functools

import jax
import jax.numpy as jnp
from jax import lax
from jax.experimental import pallas as pl
from jax.experimental.pallas import tpu as pltpu

CHUNK = 64
CHUNK_SHIFT = 6
ROPE_THETA = 500000.0
NORM_EPS = 1e-6
NEG_INF = -1e30
LOG2E = 1.4426950408889634
MLA_HEADS = 8
Q_LORA = 512
KV_LORA = 256
MLA_NOPE = 128
MLA_ROPE = 64
MLA_V = 128
MLA_QK_PAD = 256
DSA_HEADS = 8
DSA_KV_HEADS = 2
DSA_HD = 128
DSA_ROT = DSA_HD // 4
IDX_HEADS = 16
IDX_DIM = 64
IDX_ROT = IDX_DIM // 4
TOPK_MAX = 256
D_FF = 5632
CONV_W = 3

LANES = 128
VMEM_LIMIT = 56 * 1024 * 1024
INT_MIN = -2 ** 31

F32 = jnp.float32
BF16 = jnp.bfloat16


def _cparams(sem):
    return pltpu.CompilerParams(dimension_semantics=sem, vmem_limit_bytes=VMEM_LIMIT)


def _resident(shape):
    nd = len(shape)
    return pl.BlockSpec(shape, lambda *_: (0,) * nd, pipeline_mode=pl.Buffered(1))


def _rms(x, g, n=None):
    n = x.shape[-1] if n is None else n
    ms = jnp.sum(x * x, axis=-1, keepdims=True) * (1.0 / n)
    return x * lax.rsqrt(ms + NORM_EPS) * g


def _rope128(x, c, s, half):
    lane = lax.broadcasted_iota(jnp.int32, x.shape, 1)
    swapped = jnp.where(lane < half, pltpu.roll(x, LANES - half, 1), pltpu.roll(x, half, 1))
    return x * c + swapped * s


def _dot(a, b):
    return jnp.dot(a, b, preferred_element_type=F32)


def _dot_t(a, b):
    return lax.dot_general(a, b, (((1,), (1,)), ((), ())), preferred_element_type=F32)


def _in_proj_kernel(x_ref, gattn_ref, wcq_ref, wsm_ref, wqd_ref, wkvd_ref, wqi_ref, wqup_ref,
                    gqa_ref, gkva_ref, gkpe_ref, gqn_ref, gqr_ref, gdq_ref, gdk_ref,
                    c64_ref, s64_ref, c32_ref, s32_ref, c16_ref, s16_ref,
                    h_ref, qfull_ref, ckv_ref, kpe_ref, kpeb_ref, qd_ref, knew_ref, vnew_ref,
                    knewb_ref, vnewb_ref, qi_ref, ki_ref, kib_ref, wi_ref):
    x = x_ref[...]
    hb = _rms(x, gattn_ref[...]).astype(BF16)
    h_ref[...] = hb
    c64, s64 = c64_ref[...], s64_ref[...]
    c32, s32 = c32_ref[...], s32_ref[...]
    c16, s16 = c16_ref[...], s16_ref[...]

    cqn = _rms(_dot(hb, wcq_ref[...]), gqa_ref[...]).astype(BF16)
    q = _dot(cqn, wqup_ref[...])
    scale = (MLA_NOPE + MLA_ROPE) ** -0.5 * LOG2E
    for hd in range(MLA_HEADS):
        lo = hd * MLA_QK_PAD
        nope = _rms(q[:, lo:lo + LANES], gqn_ref[...]) * scale
        qfull_ref[:, lo:lo + LANES] = nope.astype(BF16)
        pe = _rms(q[:, lo + LANES:lo + 2 * LANES], gqr_ref[...], MLA_ROPE)
        pe = _rope128(pe, c64, s64, MLA_ROPE // 2) * scale
        qfull_ref[:, lo + LANES:lo + 2 * LANES] = pe.astype(BF16)

    sm = _dot(hb, wsm_ref[...])
    ckv_ref[...] = _rms(sm[:, :KV_LORA], gkva_ref[...])
    kpe = _rope128(_rms(sm[:, KV_LORA:KV_LORA + LANES], gkpe_ref[...], MLA_ROPE), c64, s64,
                   MLA_ROPE // 2)
    kpe_ref[...] = kpe[:, :MLA_ROPE]
    kpeb_ref[...] = kpe.astype(BF16)
    ki = _rope128(sm[:, KV_LORA + LANES:KV_LORA + 2 * LANES], c16, s16, IDX_ROT // 2)
    ki_ref[...] = ki[:, :IDX_DIM]
    kib_ref[...] = ki.astype(BF16)
    wi_ref[...] = sm[:, KV_LORA + 2 * LANES:KV_LORA + 2 * LANES + IDX_HEADS] * (IDX_HEADS ** -0.5)

    qd = _dot(hb, wqd_ref[...])
    for hd in range(DSA_HEADS):
        lo = hd * DSA_HD
        t = _rope128(_rms(qd[:, lo:lo + DSA_HD], gdq_ref[...]), c32, s32, DSA_ROT // 2)
        qd_ref[:, lo:lo + DSA_HD] = (t * (DSA_HD ** -0.5 * LOG2E)).astype(BF16)
    kvd = _dot(hb, wkvd_ref[...])
    for hd in range(DSA_KV_HEADS):
        lo = hd * DSA_HD
        t = _rope128(_rms(kvd[:, lo:lo + DSA_HD], gdk_ref[...]), c32, s32, DSA_ROT // 2)
        knew_ref[:, lo:lo + DSA_HD] = t
        knewb_ref[:, lo:lo + DSA_HD] = t.astype(BF16)
    v = kvd[:, DSA_KV_HEADS * DSA_HD:]
    vnew_ref[...] = v
    vnewb_ref[...] = v.astype(BF16)

    qi = _dot(hb, wqi_ref[...])
    for hd in range(IDX_HEADS):
        lo = hd * LANES
        t = _rope128(qi[:, lo:lo + LANES], c16, s16, IDX_ROT // 2)
        qi_ref[:, lo:lo + LANES] = (t * (IDX_DIM ** -0.5)).astype(BF16)


def _in_proj(xf, tabs, w, tm, t_len):
    n, d = xf.shape
    n_tiles = n // tm
    if t_len >= tm:
        per = t_len // tm
        tab_map = lambda i: (i % per, 0)
    else:
        tab_map = lambda i: (0, 0)
    row = lambda width: pl.BlockSpec((tm, width), lambda i: (i, 0))
    tab = pl.BlockSpec((tm, LANES), tab_map)
    weights = [w['attn_norm'], w['w_cq'], w['w_sm'], w['w_qd'], w['w_kvd'], w['w_qi'], w['w_qup'],
               w['q_a_norm'], w['kv_a_norm'], w['g_kpe'], w['g_qnope'], w['g_qrope'],
               w['dsa_q_norm'], w['dsa_k_norm']]
    out_widths = [(d, BF16), (MLA_HEADS * MLA_QK_PAD, BF16), (KV_LORA, F32), (MLA_ROPE, F32),
                  (LANES, BF16), (DSA_HEADS * DSA_HD, BF16), (DSA_KV_HEADS * DSA_HD, F32),
                  (DSA_KV_HEADS * DSA_HD, F32), (DSA_KV_HEADS * DSA_HD, BF16),
                  (DSA_KV_HEADS * DSA_HD, BF16), (IDX_HEADS * LANES, BF16), (IDX_DIM, F32),
                  (LANES, BF16), (IDX_HEADS, F32)]
    return pl.pallas_call(
        _in_proj_kernel,
        grid=(n_tiles,),
        in_specs=[row(d)] + [_resident(a.shape) for a in weights] + [tab] * 6,
        out_specs=[row(wd) for wd, _ in out_widths],
        out_shape=[jax.ShapeDtypeStruct((n, wd), dt) for wd, dt in out_widths],
        compiler_params=_cparams(("parallel",)),
        name="in_proj",
    )(xf, *weights, *tabs)


def _kv_up_kernel(ckv_ref, kpe_ref, w_ref, g_ref, kfull_ref, v_ref):
    kv = _dot(ckv_ref[...].astype(BF16), w_ref[...])
    kpe = kpe_ref[...]
    g = g_ref[...]
    for hd in range(MLA_HEADS):
        lo = hd * MLA_QK_PAD
        kfull_ref[:, lo:lo + LANES] = _rms(kv[:, hd * LANES:(hd + 1) * LANES], g).astype(BF16)
        kfull_ref[:, lo + LANES:lo + 2 * LANES] = kpe
    v_ref[...] = kv[:, MLA_HEADS * MLA_NOPE:].astype(BF16)


def _kv_up(ckv, kpeb, w, tm):
    n = ckv.shape[0]
    row = lambda width: pl.BlockSpec((tm, width), lambda i: (i, 0))
    return pl.pallas_call(
        _kv_up_kernel,
        grid=(n // tm,),
        in_specs=[row(KV_LORA), row(LANES), _resident(w['w_kvup'].shape),
                  _resident(w['g_knope'].shape)],
        out_specs=[row(MLA_HEADS * MLA_QK_PAD), row(MLA_HEADS * MLA_V)],
        out_shape=[jax.ShapeDtypeStruct((n, MLA_HEADS * MLA_QK_PAD), BF16),
                   jax.ShapeDtypeStruct((n, MLA_HEADS * MLA_V), BF16)],
        compiler_params=_cparams(("parallel",)),
        name="kv_up",
    )(ckv, kpeb, w['w_kvup'], w['g_knope'])


def _visible_limit(q_last_pos, s_valid):
    return jnp.minimum(((q_last_pos >> CHUNK_SHIFT) + 1) * CHUNK, s_valid)


def _softmax_step(s, v, m, l, acc):
    m_new = jnp.maximum(m, jnp.max(s, axis=-1, keepdims=True))
    alpha = jnp.exp2(m - m_new)
    p = jnp.exp2(s - m_new)
    l = alpha * l + jnp.sum(p, axis=-1, keepdims=True)
    acc = alpha * acc + _dot(p.astype(BF16), v)
    return m_new, l, acc


MXU_LOOKAHEAD = 3


def _run_tiles(tiles, score_fn, value_fn, carry):
    carry = list(carry)
    pending = [score_fn(*t) for t in tiles[:MXU_LOOKAHEAD]]
    for i, (stream, chunk) in enumerate(tiles):
        s = pending.pop(0)
        carry[stream] = _softmax_step(s, value_fn(stream, chunk), *carry[stream])
        if i + MXU_LOOKAHEAD < len(tiles):
            pending.append(score_fn(*tiles[i + MXU_LOOKAHEAD]))
    return tuple(carry)


def _mla_kernel(q_ref, k_ref, v_ref, o_ref, *, tq, tk, pos0, s_valid, nh):
    qt = pl.program_id(2)
    q_first = pos0 + qt * tq
    qchunk = (q_first + lax.broadcasted_iota(jnp.int32, (tq, 1), 0)) >> CHUNK_SHIFT
    n_chunks = pl.cdiv(_visible_limit(q_first + tq - 1, s_valid), tk)
    n_full = _visible_limit(q_first, s_valid) // tk
    qs = [q_ref[:, hd * MLA_QK_PAD:(hd + 1) * MLA_QK_PAD] for hd in range(nh)]

    def make_body(masked, unroll):
        def body(cu, carry):
            def start_of(u):
                return pl.multiple_of((cu * unroll + u) * tk, tk)

            def score(hd, u):
                start = start_of(u)
                s = _dot_t(qs[hd], k_ref[pl.ds(start, tk), hd * MLA_QK_PAD:(hd + 1) * MLA_QK_PAD])
                if masked:
                    kpos = start + lax.broadcasted_iota(jnp.int32, (1, tk), 1)
                    vis = ((kpos >> CHUNK_SHIFT) <= qchunk) & (kpos < s_valid)
                    s = jnp.where(vis, s, NEG_INF)
                return s

            def value(hd, u):
                return v_ref[pl.ds(start_of(u), tk), hd * MLA_V:(hd + 1) * MLA_V]

            tiles = [(hd, u) for u in range(unroll) for hd in range(nh)]
            return _run_tiles(tiles, score, value, carry)
        return body

    init = tuple((jnp.full((tq, 1), NEG_INF, F32), jnp.zeros((tq, 1), F32),
                  jnp.zeros((tq, MLA_V), F32)) for _ in range(nh))
    unroll = 2
    n_pairs = n_full // unroll
    carry = lax.fori_loop(0, n_pairs, make_body(False, unroll), init)
    carry = lax.fori_loop(n_pairs * unroll, n_chunks, make_body(True, 1), carry)
    for hd in range(nh):
        _, l, acc = carry[hd]
        o_ref[:, hd * MLA_V:(hd + 1) * MLA_V] = (acc / l).astype(BF16)


def _mla_attn(qfull, kfull, vfull, b, t_len, s_pad, tq, tk, pos0, s_valid, nh):
    nq = t_len // tq
    kern = functools.partial(_mla_kernel, tq=tq, tk=tk, pos0=pos0, s_valid=s_valid, nh=nh)
    return pl.pallas_call(
        kern,
        grid=(b, MLA_HEADS // nh, nq),
        in_specs=[pl.BlockSpec((tq, nh * MLA_QK_PAD), lambda bi, h, i: (bi * nq + i, h)),
                  pl.BlockSpec((s_pad, nh * MLA_QK_PAD), lambda bi, h, i: (bi, h)),
                  pl.BlockSpec((s_pad, nh * MLA_V), lambda bi, h, i: (bi, h))],
        out_specs=pl.BlockSpec((tq, nh * MLA_V), lambda bi, h, i: (bi * nq + i, h)),
        out_shape=jax.ShapeDtypeStruct((b * t_len, MLA_HEADS * MLA_V), BF16),
        compiler_params=_cparams(("parallel", "parallel", "arbitrary")),
        name="mla_attn",
    )(qfull, kfull, vfull)


def _count_rows(preds, key_scr, n_chunks, r0, rb, tk):
    def body(c, parts):
        keys = key_scr[c, r0:r0 + rb, :]
        parts = list(parts)
        for j in range(tk // LANES):
            idx = c * tk + j * LANES + lax.broadcasted_iota(jnp.int32, (1, LANES), 1)
            kj = keys[:, j * LANES:(j + 1) * LANES]
            for n, pred in enumerate(preds):
                parts[n] = parts[n] + jnp.where(pred(kj, idx), 1, 0)
        return tuple(parts)
    zero = jnp.zeros((rb, LANES), jnp.int32)
    parts = lax.fori_loop(0, n_chunks, body, (zero,) * len(preds))
    return [jnp.sum(p, axis=-1, keepdims=True) for p in parts]


def _select_rows(key_scr, bias_scr, cut_scr, n_chunks, r0, rb, tk, topk, idx_bits):
    def bit_body(b, t):
        cand = t + (jnp.int32(1) << (31 - b))
        cnt, = _count_rows([lambda kk, _: kk >= cand], key_scr, n_chunks, r0, rb, tk)
        return jnp.where(cnt >= topk, cand, t)
    thr = lax.fori_loop(0, 32, bit_body, jnp.full((rb, 1), INT_MIN, jnp.int32))

    cnt_gt, cnt_eq = _count_rows([lambda kk, _: kk > thr, lambda kk, _: kk == thr],
                                 key_scr, n_chunks, r0, rb, tk)
    need = topk - cnt_gt
    cut_scr[r0:r0 + rb, :] = jnp.full((rb, LANES), 2 ** 30, jnp.int32)
    excess = jnp.max(jnp.where((cnt_eq > need) & (thr > INT_MIN), 1, 0))

    @pl.when(excess > 0)
    def _():
        def idx_body(b, pos):
            cand = pos + (jnp.int32(1) << (idx_bits - 1 - b))
            cnt, = _count_rows([lambda kk, idx: (kk == thr) & (idx < cand)], key_scr, n_chunks,
                               r0, rb, tk)
            return jnp.where(cnt < need, cand, pos)
        pos = lax.fori_loop(0, idx_bits, idx_body, jnp.zeros((rb, 1), jnp.int32))
        cut_scr[r0:r0 + rb, :] = jnp.broadcast_to(pos, (rb, LANES))
    cut = jnp.where(thr > INT_MIN, cut_scr[r0:r0 + rb, 0:1], -1)

    def bias_body(c, _):
        keys = key_scr[c, r0:r0 + rb, :]
        idx = c * tk + lax.broadcasted_iota(jnp.int32, (1, tk), 1)
        tie = jnp.where(idx <= cut, 0.0, NEG_INF)
        bias_scr[c, r0:r0 + rb, :] = jnp.where(keys > thr, 0.0,
                                               jnp.where(keys == thr, tie, NEG_INF))
        return 0
    lax.fori_loop(0, n_chunks, bias_body, 0)


def _dsa_kernel(qi_ref, wi_ref, qd_ref, ki_ref, k_ref, v_ref, o_ref, key_scr, bias_scr, cut_scr,
                *, tq, tk, rb, streams, pos0, s_valid, topk, idx_bits):
    qt = pl.program_id(1)
    q_first = pos0 + qt * tq
    qchunk = (q_first + lax.broadcasted_iota(jnp.int32, (tq, 1), 0)) >> CHUNK_SHIFT
    n_chunks = pl.cdiv(_visible_limit(q_first + tq - 1, s_valid), tk)
    wi = wi_ref[...]

    def score_body(c, _):
        start = pl.multiple_of(c * tk, tk)
        kic = ki_ref[pl.ds(start, tk), :]
        sc = jnp.zeros((tq, tk), F32)
        for hd in range(IDX_HEADS):
            d = _dot_t(qi_ref[:, hd * LANES:(hd + 1) * LANES], kic)
            sc = sc + wi[:, hd:hd + 1] * jnp.maximum(d, 0.0)
        bits = pltpu.bitcast(sc + 0.0, jnp.int32)
        keys = bits ^ ((bits >> 31) & 0x7FFFFFFF)
        kpos = start + lax.broadcasted_iota(jnp.int32, (1, tk), 1)
        vis = ((kpos >> CHUNK_SHIFT) <= qchunk) & (kpos < s_valid)
        key_scr[c] = jnp.where(vis, keys, INT_MIN)
        return 0
    lax.fori_loop(0, n_chunks, score_body, 0)

    for r0 in range(0, tq, rb):
        _select_rows(key_scr, bias_scr, cut_scr, n_chunks, r0, rb, tk, topk, idx_bits)

    rep = DSA_HEADS // DSA_KV_HEADS
    for h0 in range(0, DSA_HEADS, streams):
        g = h0 // rep
        qs = [qd_ref[:, hd * DSA_HD:(hd + 1) * DSA_HD] for hd in range(h0, h0 + streams)]

        def att_body(c, carry, qs=qs, g=g):
            start = pl.multiple_of(c * tk, tk)

            def score(n, _):
                kc = k_ref[pl.ds(start, tk), g * DSA_HD:(g + 1) * DSA_HD]
                return _dot_t(qs[n], kc) + bias_scr[c]

            def value(n, _):
                return v_ref[pl.ds(start, tk), g * DSA_HD:(g + 1) * DSA_HD]

            return _run_tiles([(n, 0) for n in range(streams)], score, value, carry)

        init = tuple((jnp.full((tq, 1), NEG_INF, F32), jnp.zeros((tq, 1), F32),
                      jnp.zeros((tq, DSA_HD), F32)) for _ in range(streams))
        carry = lax.fori_loop(0, n_chunks, att_body, init)
        for n, (_, l, acc) in enumerate(carry):
            hd = h0 + n
            o_ref[:, hd * DSA_HD:(hd + 1) * DSA_HD] = (acc / l).astype(BF16)


def _dsa_attn(qi, wi, qd, kib, kb, vb, b, t_len, s_pad, tq, tk, pos0, s_valid, topk):
    nq = t_len // tq
    n_ch = s_pad // tk
    idx_bits = max(1, (s_pad - 1).bit_length())
    kern = functools.partial(_dsa_kernel, tq=tq, tk=tk, rb=min(tq, 128), streams=4, pos0=pos0,
                             s_valid=s_valid, topk=topk, idx_bits=idx_bits)
    qrow = lambda width: pl.BlockSpec((tq, width), lambda bi, i: (bi * nq + i, 0))
    krow = lambda width: pl.BlockSpec((s_pad, width), lambda bi, i: (bi, 0))
    return pl.pallas_call(
        kern,
        grid=(b, nq),
        in_specs=[qrow(IDX_HEADS * LANES), qrow(IDX_HEADS), qrow(DSA_HEADS * DSA_HD),
                  krow(LANES), krow(DSA_KV_HEADS * DSA_HD), krow(DSA_KV_HEADS * DSA_HD)],
        out_specs=qrow(DSA_HEADS * DSA_HD),
        out_shape=jax.ShapeDtypeStruct((b * t_len, DSA_HEADS * DSA_HD), BF16),
        scratch_shapes=[pltpu.VMEM((n_ch, tq, tk), jnp.int32), pltpu.VMEM((n_ch, tq, tk), F32),
                        pltpu.VMEM((tq, LANES), jnp.int32)],
        compiler_params=_cparams(("parallel", "arbitrary")),
        name="dsa_attn",
    )(qi, wi, qd, kib, kb, vb)


def _sigmoid(x):
    return 1.0 / (1.0 + jnp.exp(-x))


def _merge_out_kernel(h_ref, om_ref, od_ref, x_ref, wgm_ref, wgd_ref, wom_ref, wod_ref, wout_ref,
                      gffn_ref, x1_ref, hf_ref, acc_scr):
    j = pl.program_id(1)

    @pl.when(j == 0)
    def _():
        acc_scr[...] = x_ref[...]

    h = h_ref[...]
    merged = (_sigmoid(_dot(h, wgm_ref[...])) * _dot(om_ref[...], wom_ref[...])
              + _sigmoid(_dot(h, wgd_ref[...])) * _dot(od_ref[...], wod_ref[...]))
    acc_scr[...] += _dot(merged.astype(BF16), wout_ref[...])

    @pl.when(j == pl.num_programs(1) - 1)
    def _():
        x1 = acc_scr[...]
        x1_ref[...] = x1
        hf_ref[...] = _rms(x1, gffn_ref[...]).astype(BF16)


def _merge_out(h, om, od, xf, w, tm, tn):
    n, d = xf.shape
    row = lambda width: pl.BlockSpec((tm, width), lambda i, j: (i, 0))
    col = lambda rows: pl.BlockSpec((rows, tn), lambda i, j: (0, j))
    return pl.pallas_call(
        _merge_out_kernel,
        grid=(n // tm, d // tn),
        in_specs=[row(d), row(om.shape[1]), row(od.shape[1]), row(d),
                  col(d), col(d), col(om.shape[1]), col(od.shape[1]),
                  pl.BlockSpec((tn, d), lambda i, j: (j, 0)),
                  pl.BlockSpec((1, d), lambda i, j: (0, 0))],
        out_specs=[row(d), row(d)],
        out_shape=[jax.ShapeDtypeStruct((n, d), F32), jax.ShapeDtypeStruct((n, d), BF16)],
        scratch_shapes=[pltpu.VMEM((tm, d), F32)],
        compiler_params=_cparams(("parallel", "arbitrary")),
        name="merge_out",
    )(h, om, od, xf, w['w_gm'], w['w_gd'], w['w_om'], w['w_od'], w['w_out'], w['ffn_norm'])


def _gate_rows_kernel(h_ref, w_ref, o_ref):
    o_ref[...] = _dot(h_ref[...], w_ref[...])


def _gate_rows(rows, w_gate, tf):
    r, d = rows.shape
    return pl.pallas_call(
        _gate_rows_kernel,
        grid=(D_FF // tf,),
        in_specs=[pl.BlockSpec((r, d), lambda j: (0, 0)), pl.BlockSpec((d, tf), lambda j: (0, j))],
        out_specs=pl.BlockSpec((r, tf), lambda j: (0, j)),
        out_shape=jax.ShapeDtypeStruct((r, D_FF), F32),
        compiler_params=_cparams(("parallel",)),
        name="ffn_gate_rows",
    )(rows, w_gate)


def _split3(x):
    hi = x.astype(BF16)
    r1 = x - hi.astype(F32)
    mid = r1.astype(BF16)
    lo = (r1 - mid.astype(F32)).astype(BF16)
    return hi, mid, lo


def _ffn_kernel(hf_ref, x1_ref, halo_ref, wg_ref, wu_ref, wd_ref, cw_ref, cb_ref, y_ref, acc_scr,
                *, tm, t_seg, halo_rows):
    j = pl.program_id(1)

    @pl.when(j == 0)
    def _():
        acc_scr[...] = x1_ref[...]

    hf = hf_ref[...]
    g = _dot(hf, wg_ref[...])
    up = _dot(hf, wu_ref[...])

    seg_shift = t_seg.bit_length() - 1
    row = lax.broadcasted_iota(jnp.int32, (tm, halo_rows), 0)
    colh = lax.broadcasted_iota(jnp.int32, (tm, halo_rows), 1)
    seg2 = (row >> seg_shift) * 2
    r_in = row & (t_seg - 1)
    e1 = jnp.where((r_in == 0) & (colh == seg2 + 1), 1.0, 0.0).astype(BF16)
    e2 = jnp.where(((r_in == 0) & (colh == seg2)) | ((r_in == 1) & (colh == seg2 + 1)),
                   1.0, 0.0).astype(BF16)
    pieces = _split3(halo_ref[0])
    fix1 = _dot(e1, pieces[0]) + _dot(e1, pieces[1]) + _dot(e1, pieces[2])
    fix2 = _dot(e2, pieces[0]) + _dot(e2, pieces[1]) + _dot(e2, pieces[2])
    r1 = lax.broadcasted_iota(jnp.int32, (tm, 1), 0) & (t_seg - 1)
    g1 = jnp.where(r1 < 1, fix1, pltpu.roll(g, 1, 0))
    g2 = jnp.where(r1 < 2, fix2, pltpu.roll(g, 2, 0))
    cw = cw_ref[...]
    conv = cb_ref[...] + cw[0:1, :] * g2 + cw[1:2, :] * g1 + cw[2:3, :] * g
    act = conv * _sigmoid(conv) * up
    acc_scr[...] += _dot(act.astype(BF16), wd_ref[...])

    @pl.when(j == pl.num_programs(1) - 1)
    def _():
        y_ref[...] = acc_scr[...]


def _ffn(hf, x1, halo, w, tm, tf, t_seg):
    n, d = x1.shape
    halo_rows = halo.shape[1]
    kern = functools.partial(_ffn_kernel, tm=tm, t_seg=t_seg, halo_rows=halo_rows)
    row = lambda width: pl.BlockSpec((tm, width), lambda i, j: (i, 0))
    return pl.pallas_call(
        kern,
        grid=(n // tm, D_FF // tf),
        in_specs=[row(d), row(d),
                  pl.BlockSpec((1, halo_rows, tf), lambda i, j: (i, 0, j)),
                  pl.BlockSpec((d, tf), lambda i, j: (0, j)),
                  pl.BlockSpec((d, tf), lambda i, j: (0, j)),
                  pl.BlockSpec((tf, d), lambda i, j: (j, 0)),
                  pl.BlockSpec((CONV_W, tf), lambda i, j: (0, j)),
                  pl.BlockSpec((1, tf), lambda i, j: (0, j))],
        out_specs=row(d),
        out_shape=jax.ShapeDtypeStruct((n, d), F32),
        scratch_shapes=[pltpu.VMEM((tm, d), F32)],
        compiler_params=_cparams(("parallel", "arbitrary")),
        name="conv_ffn",
    )(hf, x1, halo, w['w_ffn_gate'], w['w_ffn_upv'], w['w_ffn_down'], w['conv_w'], w['conv_b'])


def _pad_cols(a, width):
    return jnp.pad(a, ((0, 0), (0, width - a.shape[1])))


def _prep_weights(attn_norm, w_in, q_a_norm, w_q_up, kv_a_norm, w_kv_up, mla_q_nope_norm,
                  mla_q_rope_norm, mla_k_nope_norm, mla_k_rope_norm, dsa_q_norm, dsa_k_norm,
                  w_o_mla, w_o_dsa, w_out, ffn_norm, w_ffn_up, conv_w, conv_b, w_ffn_down):
    d = w_in.shape[0]
    widths = (Q_LORA, KV_LORA + MLA_ROPE, DSA_HEADS * DSA_HD, 2 * DSA_KV_HEADS * DSA_HD,
              IDX_HEADS * IDX_DIM, IDX_DIM, IDX_HEADS, 2 * d)
    offs = [0]
    for wd in widths:
        offs.append(offs[-1] + wd)
    seg = [w_in[:, offs[i]:offs[i + 1]] for i in range(len(widths))]
    w = {}
    w['w_cq'] = seg[0].astype(BF16)
    w['w_sm'] = jnp.concatenate(
        [seg[1][:, :KV_LORA], _pad_cols(seg[1][:, KV_LORA:], LANES), _pad_cols(seg[5], LANES),
         _pad_cols(seg[6], LANES)], axis=1).astype(BF16)
    w['w_qd'] = seg[2].astype(BF16)
    w['w_kvd'] = seg[3].astype(BF16)
    w['w_qi'] = jnp.pad(seg[4].reshape(d, IDX_HEADS, IDX_DIM),
                        ((0, 0), (0, 0), (0, LANES - IDX_DIM))).reshape(d, IDX_HEADS * LANES).astype(BF16)
    w['w_gm'] = seg[7][:, :d].astype(BF16)
    w['w_gd'] = seg[7][:, d:].astype(BF16)
    qup = w_q_up.reshape(Q_LORA, MLA_HEADS, MLA_NOPE + MLA_ROPE)
    w['w_qup'] = jnp.pad(qup, ((0, 0), (0, 0), (0, MLA_QK_PAD - MLA_NOPE - MLA_ROPE))
                         ).reshape(Q_LORA, MLA_HEADS * MLA_QK_PAD).astype(BF16)
    kvup = w_kv_up.reshape(KV_LORA, MLA_HEADS, MLA_NOPE + MLA_V)
    w['w_kvup'] = jnp.concatenate(
        [kvup[:, :, :MLA_NOPE].reshape(KV_LORA, MLA_HEADS * MLA_NOPE),
         kvup[:, :, MLA_NOPE:].reshape(KV_LORA, MLA_HEADS * MLA_V)], axis=1).astype(BF16)
    r2 = lambda a: a.reshape(1, -1).astype(F32)
    w['attn_norm'] = r2(attn_norm)
    w['q_a_norm'] = r2(q_a_norm)
    w['kv_a_norm'] = r2(kv_a_norm)
    w['g_kpe'] = _pad_cols(r2(mla_k_rope_norm), LANES)
    w['g_qnope'] = r2(mla_q_nope_norm)
    w['g_qrope'] = _pad_cols(r2(mla_q_rope_norm), LANES)
    w['g_knope'] = r2(mla_k_nope_norm)
    w['dsa_q_norm'] = r2(dsa_q_norm)
    w['dsa_k_norm'] = r2(dsa_k_norm)
    w['ffn_norm'] = r2(ffn_norm)
    w['w_om'] = w_o_mla.astype(BF16)
    w['w_od'] = w_o_dsa.astype(BF16)
    w['w_out'] = w_out.astype(BF16)
    w['w_ffn_gate'] = w_ffn_up[:, :D_FF].astype(BF16)
    w['w_ffn_upv'] = w_ffn_up[:, D_FF:].astype(BF16)
    w['w_ffn_down'] = w_ffn_down.astype(BF16)
    w['conv_w'] = conv_w.astype(F32)
    w['conv_b'] = r2(conv_b)
    return w


def _rope_tables(pos, rot):
    half = rot // 2
    inv = ROPE_THETA ** (-jnp.arange(half, dtype=F32) / half)
    ang = pos.astype(F32)[:, None] * inv[None, :]
    cos, sin = jnp.cos(ang), jnp.sin(ang)
    t = pos.shape[0]
    c = jnp.concatenate([cos, cos, jnp.ones((t, LANES - rot), F32)], axis=1)
    s = jnp.concatenate([-sin, sin, jnp.zeros((t, LANES - rot), F32)], axis=1)
    return c, s


def _round_up(a, m):
    return (a + m - 1) // m * m


def _tile(n, target):
    t = min(n, target)
    assert n % t == 0, (n, t)
    return t


def _layer(x, pos0, past, w):
    b, t_len, d = x.shape
    n = b * t_len
    xf = x.reshape(n, d)

    tm = _tile(n, 512)
    tm_in = _tile(n, 256)
    assert tm % t_len == 0 or t_len % tm == 0
    assert tm_in % t_len == 0 or t_len % tm_in == 0
    pos = pos0 + jnp.arange(t_len, dtype=jnp.int32)
    tabs = []
    for rot in (MLA_ROPE, DSA_ROT, IDX_ROT):
        c, s = _rope_tables(pos, rot)
        if t_len < tm_in:
            c, s = jnp.tile(c, (tm_in // t_len, 1)), jnp.tile(s, (tm_in // t_len, 1))
        tabs += [c, s]

    (h, qfull, ckv, kpe, kpeb, qd, knew, vnew, knewb, vnewb, qi, ki, kib, wi) = _in_proj(
        xf, tabs, w, tm_in, t_len)

    if past is None:
        s_len = t_len
        tk = _tile(s_len, 512)
        s_pad = s_len
        all_ckv, all_kpeb, all_kb, all_vb, all_kib = ckv, kpeb, knewb, vnewb, kib
        conv_hist = jnp.zeros((b, CONV_W - 1, D_FF), F32)
    else:
        p_ckv, p_kpe, p_k, p_v, p_ki, conv_hist = past
        past_len = p_ckv.shape[1]
        s_len = past_len + t_len
        tk = 384 if s_len > 384 else _round_up(s_len, LANES)
        s_pad = _round_up(s_len, tk)
        padr = s_pad - s_len

        def cat(p_arr, new, width, dtype, lane_pad=0):
            p2 = p_arr.reshape(b, past_len, width).astype(dtype)
            if lane_pad:
                p2 = jnp.pad(p2, ((0, 0), (0, 0), (0, lane_pad)))
            a = jnp.concatenate([p2, new.reshape(b, t_len, width + lane_pad)], axis=1)
            return jnp.pad(a, ((0, 0), (0, padr), (0, 0))).reshape(b * s_pad, width + lane_pad)

        all_ckv = cat(p_ckv, ckv, KV_LORA, F32)
        all_kpeb = cat(p_kpe, kpeb, MLA_ROPE, BF16, LANES - MLA_ROPE)
        all_kb = cat(p_k, knewb, DSA_KV_HEADS * DSA_HD, BF16)
        all_vb = cat(p_v, vnewb, DSA_KV_HEADS * DSA_HD, BF16)
        all_kib = cat(p_ki, kib, IDX_DIM, BF16, LANES - IDX_DIM)

    kfull, vfull = _kv_up(all_ckv, all_kpeb, w, _tile(b * s_pad, 512 if past is None else tk))
    tq = _tile(t_len, 256)
    o_mla = _mla_attn(qfull, kfull, vfull, b, t_len, s_pad, tq, tk, pos0, s_len, 2)
    topk = min(TOPK_MAX, s_len // 4)
    o_dsa = _dsa_attn(qi, wi, qd, all_kib, all_kb, all_vb, b, t_len, s_pad, tq, tk, pos0, s_len,
                      topk)

    x1, hf = _merge_out(h, o_mla, o_dsa, xf, w, tm, 512)

    t_seg = min(t_len, tm)
    assert t_seg & (t_seg - 1) == 0 and t_seg >= CONV_W - 1
    n_t = t_len // t_seg
    segs = tm // t_seg
    brows = hf.reshape(b, n_t, t_seg, d)[:, :, t_seg - (CONV_W - 1):, :].reshape(-1, d)
    bg = _gate_rows(brows, w['w_ffn_gate'], 512).reshape(b, n_t, CONV_W - 1, D_FF)
    halo = jnp.concatenate([conv_hist.astype(F32)[:, None], bg[:, :-1]], axis=1)
    halo = halo.reshape(n // tm, segs * (CONV_W - 1), D_FF)
    halo_rows = _round_up(halo.shape[1], 16)
    halo = jnp.pad(halo, ((0, 0), (0, halo_rows - halo.shape[1]), (0, 0)))
    y = _ffn(hf, x1, halo, w, tm, 512, t_seg)

    state = (ckv.reshape(b, t_len, KV_LORA), kpe.reshape(b, t_len, MLA_ROPE),
             knew.reshape(b, t_len, DSA_KV_HEADS, DSA_HD), vnew.reshape(b, t_len, DSA_KV_HEADS, DSA_HD),
             ki.reshape(b, t_len, IDX_DIM), bg[:, -1])
    return y.reshape(b, t_len, d), state


def kernel(x_prompt, x_sample, cache_mla_ckv, cache_mla_kpe, cache_dsa_k, cache_dsa_v, cache_idx_k, state_ffn_conv, attn_norm, w_in, q_a_norm, w_q_up, kv_a_norm, w_kv_up, mla_q_nope_norm, mla_q_rope_norm, mla_k_nope_norm, mla_k_rope_norm, dsa_q_norm, dsa_k_norm, w_o_mla, w_o_dsa, w_out, ffn_norm, w_ffn_up, conv_w, conv_b, w_ffn_down):
    depth = w_in.shape[0]
    past_len = cache_mla_ckv.shape[2]
    xp, xs = x_prompt, x_sample
    p_states, s_states = [], []
    for l in range(depth):
        w = _prep_weights(attn_norm[l], w_in[l], q_a_norm[l], w_q_up[l], kv_a_norm[l], w_kv_up[l],
                          mla_q_nope_norm[l], mla_q_rope_norm[l], mla_k_nope_norm[l],
                          mla_k_rope_norm[l], dsa_q_norm[l], dsa_k_norm[l], w_o_mla[l], w_o_dsa[l],
                          w_out[l], ffn_norm[l], w_ffn_up[l], conv_w[l], conv_b[l], w_ffn_down[l])
        past = (cache_mla_ckv[l], cache_mla_kpe[l], cache_dsa_k[l], cache_dsa_v[l],
                cache_idx_k[l], state_ffn_conv[l])
        xp, sp = _layer(xp, 0, None, w)
        xs, ss = _layer(xs, past_len, past, w)
        p_states.append(sp)
        s_states.append(ss)
    p_out = [jnp.stack(a) for a in zip(*p_states)]
    s_out = [jnp.stack(a) for a in zip(*s_states)]
    return (xp, xs, *p_out, *s_out)
```

```python
import functools

import jax
import jax.numpy as jnp
from jax import lax
from jax.experimental import pallas as pl
from jax.experimental.pallas import tpu as pltpu

CHUNK = 64
CHUNK_SHIFT = 6
ROPE_THETA = 500000.0
NORM_EPS = 1e-6
NEG_INF = -1e30
LOG2E = 1.4426950408889634
MLA_HEADS = 8
Q_LORA = 512
KV_LORA = 256
MLA_NOPE = 128
MLA_ROPE = 64
MLA_V = 128
MLA_QK_PAD = 256
DSA_HEADS = 8
DSA_KV_HEADS = 2
DSA_HD = 128
DSA_ROT = DSA_HD // 4
IDX_HEADS = 16
IDX_DIM = 64
IDX_ROT = IDX_DIM // 4
TOPK_MAX = 256
D_FF = 5632
CONV_W = 3
MLA_LOOKAHEAD = 6
DSA_LOOKAHEAD = 4

LANES = 128
SUBLANES = 8
VMEM_LIMIT = 56 * 1024 * 1024
INT_MIN = -2 ** 31

F32 = jnp.float32
BF16 = jnp.bfloat16


def _cparams(sem, flags=None):
    return pltpu.CompilerParams(dimension_semantics=sem, vmem_limit_bytes=VMEM_LIMIT, flags=flags)


def _resident(shape):
    nd = len(shape)
    return pl.BlockSpec(shape, lambda *_: (0,) * nd, pipeline_mode=pl.Buffered(1))


def _rms(x, g, n=None):
    n = x.shape[-1] if n is None else n
    ms = jnp.sum(x * x, axis=-1, keepdims=True) * (1.0 / n)
    return x * lax.rsqrt(ms + NORM_EPS) * g


def _rope128(x, c, s, half):
    lane = lax.broadcasted_iota(jnp.int32, x.shape, 1)
    swapped = jnp.where(lane < half, pltpu.roll(x, LANES - half, 1), pltpu.roll(x, half, 1))
    return x * c + swapped * s


def _dot(a, b):
    return jnp.dot(a, b, preferred_element_type=F32)


def _dot_t(a, b):
    return lax.dot_general(a, b, (((1,), (1,)), ((), ())), preferred_element_type=F32)


def _in_proj_kernel(x_ref, gattn_ref, wcq_ref, wsm_ref, wqd_ref, wkvd_ref, wqi_ref, wqup_ref, wwit_ref,
                    gqa_ref, gkva_ref, gkpe_ref, gqn_ref, gqr_ref, gdq_ref, gdk_ref,
                    c64_ref, s64_ref, c32_ref, s32_ref, c16_ref, s16_ref,
                    h_ref, qfull_ref, ckv_ref, kpe_ref, kpeb_ref, qd_ref, knew_ref, vnew_ref,
                    knewb_ref, vnewb_ref, qi_ref, ki_ref, kib_ref, wit_ref):
    x = x_ref[...]
    hb = _rms(x, gattn_ref[...]).astype(BF16)
    h_ref[...] = hb
    c64, s64 = c64_ref[...], s64_ref[...]
    c32, s32 = c32_ref[...], s32_ref[...]
    c16, s16 = c16_ref[...], s16_ref[...]

    cqn = _rms(_dot(hb, wcq_ref[...]), gqa_ref[...]).astype(BF16)
    q = _dot(cqn, wqup_ref[...])
    scale = (MLA_NOPE + MLA_ROPE) ** -0.5 * LOG2E
    for hd in range(MLA_HEADS):
        lo = hd * MLA_QK_PAD
        nope = _rms(q[:, lo:lo + LANES], gqn_ref[...]) * scale
        qfull_ref[:, lo:lo + LANES] = nope.astype(BF16)
        pe = _rms(q[:, lo + LANES:lo + 2 * LANES], gqr_ref[...], MLA_ROPE)
        pe = _rope128(pe, c64, s64, MLA_ROPE // 2) * scale
        qfull_ref[:, lo + LANES:lo + 2 * LANES] = pe.astype(BF16)

    sm = _dot(hb, wsm_ref[...])
    ckv_ref[...] = _rms(sm[:, :KV_LORA], gkva_ref[...])
    kpe = _rope128(_rms(sm[:, KV_LORA:KV_LORA + LANES], gkpe_ref[...], MLA_ROPE), c64, s64,
                   MLA_ROPE // 2)
    kpe_ref[...] = kpe[:, :MLA_ROPE]
    kpeb_ref[...] = kpe.astype(BF16)
    ki = _rope128(sm[:, KV_LORA + LANES:KV_LORA + 2 * LANES], c16, s16, IDX_ROT // 2)
    ki_ref[...] = ki[:, :IDX_DIM]
    kib_ref[...] = ki.astype(BF16)
    wit_ref[...] = _dot_t(wwit_ref[...], hb) * (IDX_HEADS ** -0.5)

    qd = _dot(hb, wqd_ref[...])
    for hd in range(DSA_HEADS):
        lo = hd * DSA_HD
        t = _rope128(_rms(qd[:, lo:lo + DSA_HD], gdq_ref[...]), c32, s32, DSA_ROT // 2)
        qd_ref[:, lo:lo + DSA_HD] = (t * (DSA_HD ** -0.5 * LOG2E)).astype(BF16)
    kvd = _dot(hb, wkvd_ref[...])
    for hd in range(DSA_KV_HEADS):
        lo = hd * DSA_HD
        t = _rope128(_rms(kvd[:, lo:lo + DSA_HD], gdk_ref[...]), c32, s32, DSA_ROT // 2)
        knew_ref[:, lo:lo + DSA_HD] = t
        knewb_ref[:, lo:lo + DSA_HD] = t.astype(BF16)
    v = kvd[:, DSA_KV_HEADS * DSA_HD:]
    vnew_ref[...] = v
    vnewb_ref[...] = v.astype(BF16)

    qi = _dot(hb, wqi_ref[...])
    for hd in range(IDX_HEADS):
        lo = hd * LANES
        t = _rope128(qi[:, lo:lo + LANES], c16, s16, IDX_ROT // 2)
        qi_ref[:, lo:lo + LANES] = (t * (IDX_DIM ** -0.5)).astype(BF16)


def _in_proj(xf, tabs, w, tm, t_len):
    n, d = xf.shape
    n_tiles = n // tm
    if t_len >= tm:
        per = t_len // tm
        tab_map = lambda i: (i % per, 0)
    else:
        tab_map = lambda i: (0, 0)
    row = lambda width: pl.BlockSpec((tm, width), lambda i: (i, 0))
    tab = pl.BlockSpec((tm, LANES), tab_map)
    weights = [w['attn_norm'], w['w_cq'], w['w_sm'], w['w_qd'], w['w_kvd'], w['w_qi'], w['w_qup'],
               w['w_wit'],
               w['q_a_norm'], w['kv_a_norm'], w['g_kpe'], w['g_qnope'], w['g_qrope'],
               w['dsa_q_norm'], w['dsa_k_norm']]
    out_widths = [(d, BF16), (MLA_HEADS * MLA_QK_PAD, BF16), (KV_LORA, F32), (MLA_ROPE, F32),
                  (LANES, BF16), (DSA_HEADS * DSA_HD, BF16), (DSA_KV_HEADS * DSA_HD, F32),
                  (DSA_KV_HEADS * DSA_HD, F32), (DSA_KV_HEADS * DSA_HD, BF16),
                  (DSA_KV_HEADS * DSA_HD, BF16), (IDX_HEADS * LANES, BF16), (IDX_DIM, F32),
                  (LANES, BF16)]
    return pl.pallas_call(
        _in_proj_kernel,
        grid=(n_tiles,),
        in_specs=[row(d)] + [_resident(a.shape) for a in weights] + [tab] * 6,
        out_specs=[row(wd) for wd, _ in out_widths]
        + [pl.BlockSpec((IDX_HEADS, tm), lambda i: (0, i))],
        out_shape=[jax.ShapeDtypeStruct((n, wd), dt) for wd, dt in out_widths]
        + [jax.ShapeDtypeStruct((IDX_HEADS, n), F32)],
        compiler_params=_cparams(("parallel",)),
        name="in_proj",
    )(xf, *weights, *tabs)


def _kv_up_kernel(ckv_ref, kpe_ref, w_ref, g_ref, kfull_ref, vt_ref):
    kv = _dot(ckv_ref[...].astype(BF16), w_ref[...])
    kpe = kpe_ref[...]
    g = g_ref[...]
    for hd in range(MLA_HEADS):
        lo = hd * MLA_QK_PAD
        kfull_ref[:, lo:lo + LANES] = _rms(kv[:, hd * LANES:(hd + 1) * LANES], g).astype(BF16)
        kfull_ref[:, lo + LANES:lo + 2 * LANES] = kpe
    for hd in range(MLA_HEADS):
        lo = MLA_HEADS * MLA_NOPE + hd * MLA_V
        vt_ref[0, hd] = kv[:, lo:lo + MLA_V].T.astype(BF16)


def _kv_up(ckv, kpeb, w, tm):
    n = ckv.shape[0]
    row = lambda width: pl.BlockSpec((tm, width), lambda i: (i, 0))
    return pl.pallas_call(
        _kv_up_kernel,
        grid=(n // tm,),
        in_specs=[row(KV_LORA), row(LANES), _resident(w['w_kvup'].shape),
                  _resident(w['g_knope'].shape)],
        out_specs=[row(MLA_HEADS * MLA_QK_PAD),
                   pl.BlockSpec((1, MLA_HEADS, MLA_V, tm), lambda i: (i, 0, 0, 0))],
        out_shape=[jax.ShapeDtypeStruct((n, MLA_HEADS * MLA_QK_PAD), BF16),
                   jax.ShapeDtypeStruct((n // tm, MLA_HEADS, MLA_V, tm), BF16)],
        compiler_params=_cparams(("parallel",)),
        name="kv_up",
    )(ckv, kpeb, w['w_kvup'], w['g_knope'])


def _visible_limit(q_last_pos, s_valid):
    return jnp.minimum(((q_last_pos >> CHUNK_SHIFT) + 1) * CHUNK, s_valid)


def _softmax_step(s, vt, m, l, acc):
    m_new = jnp.maximum(m, jnp.max(s, axis=0, keepdims=True))
    alpha = jnp.exp2(m - m_new)
    p = jnp.exp2(s - m_new)
    l = alpha * l + jnp.sum(p, axis=0, keepdims=True)
    acc = alpha * acc + _dot(vt, p.astype(BF16))
    return m_new, l, acc


def _run_tiles(tiles, score_fn, value_fn, carry, lookahead):
    carry = list(carry)
    pending = [score_fn(*t) for t in tiles[:lookahead]]
    for i, (stream, chunk) in enumerate(tiles):
        s = pending.pop(0)
        carry[stream] = _softmax_step(s, value_fn(stream, chunk), *carry[stream])
        if i + lookahead < len(tiles):
            pending.append(score_fn(*tiles[i + lookahead]))
    return tuple(carry)


def _softmax_init(tq, dv, streams):
    return tuple((jnp.full((1, tq), NEG_INF, F32), jnp.zeros((1, tq), F32),
                  jnp.zeros((dv, tq), F32)) for _ in range(streams))


def _lane_width(tq):
    return max(tq, LANES)


def _transposed(block):
    x = block.astype(F32)
    tq, d = x.shape
    if tq < _lane_width(tq):
        x = jnp.concatenate([x, jnp.zeros((_lane_width(tq) - tq, d), F32)], axis=0)
    return x.T.astype(BF16)


def _visibility(start, tk, qchunk, s_valid):
    kpos = start + lax.broadcasted_iota(jnp.int32, (tk, 1), 0)
    return ((kpos >> CHUNK_SHIFT) <= qchunk) & (kpos < s_valid)


def _mla_kernel(q_ref, k_ref, vt_ref, o_ref, *, tq, tk, pos0, s_valid, nh):
    qt = pl.program_id(2)
    q_first = pos0 + qt * tq
    tql = _lane_width(tq)
    qchunk = (q_first + lax.broadcasted_iota(jnp.int32, (1, tql), 1)) >> CHUNK_SHIFT
    n_chunks = pl.cdiv(_visible_limit(q_first + tq - 1, s_valid), tk)
    n_full = _visible_limit(q_first, s_valid) // tk
    qts = [_transposed(q_ref[:, hd * MLA_QK_PAD:(hd + 1) * MLA_QK_PAD]) for hd in range(nh)]

    def make_body(masked, unroll):
        def body(cu, carry):
            def score(hd, u):
                start = pl.multiple_of((cu * unroll + u) * tk, tk)
                s = _dot(k_ref[pl.ds(start, tk), hd * MLA_QK_PAD:(hd + 1) * MLA_QK_PAD], qts[hd])
                if masked:
                    s = jnp.where(_visibility(start, tk, qchunk, s_valid), s, NEG_INF)
                return s

            def value(hd, u):
                return vt_ref[cu * unroll + u, hd]

            tiles = [(hd, u) for u in range(unroll) for hd in range(nh)]
            return _run_tiles(tiles, score, value, carry, MLA_LOOKAHEAD)
        return body

    unroll = 2
    n_pairs = n_full // unroll
    carry = lax.fori_loop(0, n_pairs, make_body(False, unroll), _softmax_init(tql, MLA_V, nh))
    carry = lax.fori_loop(n_pairs * unroll, n_chunks, make_body(True, 1), carry)
    for hd in range(nh):
        _, l, acc = carry[hd]
        o_ref[:, hd * MLA_V:(hd + 1) * MLA_V] = (acc / l).T[:tq].astype(BF16)


def _mla_attn(qfull, kfull, vt, b, t_len, s_pad, tq, tk, pos0, s_valid, nh):
    nq = t_len // tq
    n_ch = s_pad // tk
    kern = functools.partial(_mla_kernel, tq=tq, tk=tk, pos0=pos0, s_valid=s_valid, nh=nh)
    return pl.pallas_call(
        kern,
        grid=(b, MLA_HEADS // nh, nq),
        in_specs=[pl.BlockSpec((tq, nh * MLA_QK_PAD), lambda bi, h, i: (bi * nq + i, h)),
                  pl.BlockSpec((s_pad, nh * MLA_QK_PAD), lambda bi, h, i: (bi, h)),
                  pl.BlockSpec((n_ch, nh, MLA_V, tk), lambda bi, h, i: (bi, h, 0, 0))],
        out_specs=pl.BlockSpec((tq, nh * MLA_V), lambda bi, h, i: (bi * nq + i, h)),
        out_shape=jax.ShapeDtypeStruct((b * t_len, MLA_HEADS * MLA_V), BF16),
        compiler_params=_cparams(("parallel", "parallel", "arbitrary")),
        name="mla_attn",
    )(qfull, kfull, vt)


def _count_keys(preds, key_scr, n_chunks, tq, tk):
    def body(c, parts):
        keys = key_scr[c]
        idx = c * tk + lax.broadcasted_iota(jnp.int32, (tk, 1), 0)
        out = []
        for part, pred in zip(parts, preds):
            hit = jnp.where(pred(keys, idx), 1, 0).reshape(tk // SUBLANES, SUBLANES, tq)
            out.append(part + jnp.sum(hit, axis=0))
        return tuple(out)
    zero = jnp.zeros((SUBLANES, tq), jnp.int32)
    parts = lax.fori_loop(0, n_chunks, body, (zero,) * len(preds))
    return [jnp.sum(p, axis=0, keepdims=True) for p in parts]


def _dsa_kernel(qi_ref, wit_ref, qd_ref, ki_ref, k_ref, vt_ref, o_ref, key_scr, bias_scr, cut_scr,
                *, tq, tk, streams, pos0, s_valid, topk, idx_bits):
    qt = pl.program_id(1)
    q_first = pos0 + qt * tq
    tql = _lane_width(tq)
    lane = lax.broadcasted_iota(jnp.int32, (1, tql), 1)
    qchunk = (q_first + lane) >> CHUNK_SHIFT
    n_chunks = pl.cdiv(_visible_limit(q_first + tq - 1, s_valid), tk)
    wit = wit_ref[0]
    if tq < tql:
        wit = jnp.concatenate([wit, jnp.zeros((IDX_HEADS, tql - tq), F32)], axis=1)
    qits = [_transposed(qi_ref[:, hd * LANES:(hd + 1) * LANES]) for hd in range(IDX_HEADS)]

    def score_body(c, _):
        start = pl.multiple_of(c * tk, tk)
        kic = ki_ref[pl.ds(start, tk), :]
        sc = jnp.zeros((tk, tql), F32)
        for hd in range(IDX_HEADS):
            sc = sc + wit[hd:hd + 1, :] * jnp.maximum(_dot(kic, qits[hd]), 0.0)
        bits = pltpu.bitcast(sc + 0.0, jnp.int32)
        keys = bits ^ ((bits >> 31) & 0x7FFFFFFF)
        key_scr[c] = jnp.where(_visibility(start, tk, qchunk, s_valid), keys, INT_MIN)
        return 0
    lax.fori_loop(0, n_chunks, score_body, 0)

    def bit_body(b, t):
        cand = t + (jnp.int32(1) << (31 - b))
        cnt, = _count_keys([lambda kk, _: kk >= cand], key_scr, n_chunks, tql, tk)
        return jnp.where(cnt >= topk, cand, t)
    thr = lax.fori_loop(0, 32, bit_body, jnp.full((1, tql), INT_MIN, jnp.int32))

    cnt_gt, cnt_eq = _count_keys([lambda kk, _: kk > thr, lambda kk, _: kk == thr],
                                 key_scr, n_chunks, tql, tk)
    need = topk - cnt_gt
    cut_scr[...] = jnp.full((SUBLANES, tql), 2 ** 30, jnp.int32)
    excess = jnp.max(jnp.where((cnt_eq > need) & (thr > INT_MIN) & (lane < tq), 1, 0))

    @pl.when(excess > 0)
    def _():
        def idx_body(b, pos):
            cand = pos + (jnp.int32(1) << (idx_bits - 1 - b))
            cnt, = _count_keys([lambda kk, idx: (kk == thr) & (idx < cand)], key_scr, n_chunks,
                               tql, tk)
            return jnp.where(cnt < need, cand, pos)
        pos = lax.fori_loop(0, idx_bits, idx_body, jnp.zeros((1, tql), jnp.int32))
        cut_scr[...] = jnp.broadcast_to(pos, (SUBLANES, tql))
    cut = jnp.where(thr > INT_MIN, cut_scr[0:1, :], -1)

    def bias_body(c, _):
        keys = key_scr[c]
        idx = c * tk + lax.broadcasted_iota(jnp.int32, (tk, 1), 0)
        tie = jnp.where(idx <= cut, 0.0, NEG_INF)
        bias_scr[c] = jnp.where(keys > thr, 0.0, jnp.where(keys == thr, tie, NEG_INF))
        return 0
    lax.fori_loop(0, n_chunks, bias_body, 0)

    rep = DSA_HEADS // DSA_KV_HEADS
    for h0 in range(0, DSA_HEADS, streams):
        qts = [_transposed(qd_ref[:, hd * DSA_HD:(hd + 1) * DSA_HD]) for hd in range(h0, h0 + streams)]

        def att_body(c, carry, qts=qts, h0=h0):
            start = pl.multiple_of(c * tk, tk)

            def score(n, _):
                g = (h0 + n) // rep
                kc = k_ref[pl.ds(start, tk), g * DSA_HD:(g + 1) * DSA_HD]
                return _dot(kc, qts[n]) + bias_scr[c]

            def value(n, _):
                return vt_ref[c, (h0 + n) // rep]

            return _run_tiles([(n, 0) for n in range(streams)], score, value, carry,
                              DSA_LOOKAHEAD)

        carry = lax.fori_loop(0, n_chunks, att_body, _softmax_init(tql, DSA_HD, streams))
        for n, (_, l, acc) in enumerate(carry):
            hd = h0 + n
            o_ref[:, hd * DSA_HD:(hd + 1) * DSA_HD] = (acc / l).T[:tq].astype(BF16)


def _dsa_attn(qi, wit, qd, kib, kb, vt, b, t_len, s_pad, tq, tk, pos0, s_valid, topk):
    nq = t_len // tq
    n_ch = s_pad // tk
    idx_bits = max(1, (s_pad - 1).bit_length())
    tql = _lane_width(tq)
    wit = wit.reshape(IDX_HEADS, b * nq, tq).transpose(1, 0, 2)
    kern = functools.partial(_dsa_kernel, tq=tq, tk=tk, streams=8, pos0=pos0, s_valid=s_valid,
                             topk=topk, idx_bits=idx_bits)
    qrow = lambda width: pl.BlockSpec((tq, width), lambda bi, i: (bi * nq + i, 0))
    krow = lambda width: pl.BlockSpec((s_pad, width), lambda bi, i: (bi, 0))
    return pl.pallas_call(
        kern,
        grid=(b, nq),
        in_specs=[qrow(IDX_HEADS * LANES),
                  pl.BlockSpec((1, IDX_HEADS, tq), lambda bi, i: (bi * nq + i, 0, 0)),
                  qrow(DSA_HEADS * DSA_HD), krow(LANES), krow(DSA_KV_HEADS * DSA_HD),
                  pl.BlockSpec((n_ch, DSA_KV_HEADS, DSA_HD, tk), lambda bi, i: (bi, 0, 0, 0))],
        out_specs=qrow(DSA_HEADS * DSA_HD),
        out_shape=jax.ShapeDtypeStruct((b * t_len, DSA_HEADS * DSA_HD), BF16),
        scratch_shapes=[pltpu.VMEM((n_ch, tk, tql), jnp.int32), pltpu.VMEM((n_ch, tk, tql), F32),
                        pltpu.VMEM((SUBLANES, tql), jnp.int32)],
        compiler_params=_cparams(("parallel", "arbitrary")),
        name="dsa_attn",
    )(qi, wit, qd, kib, kb, vt)


def _sigmoid(x):
    return 1.0 / (1.0 + jnp.exp(-x))


def _merge_out_kernel(h_ref, om_ref, od_ref, x_ref, wgm_ref, wgd_ref, wom_ref, wod_ref, wout_ref,
                      gffn_ref, x1_ref, hf_ref, acc_scr):
    j = pl.program_id(1)

    @pl.when(j == 0)
    def _():
        acc_scr[...] = x_ref[...]

    h = h_ref[...]
    merged = (_sigmoid(_dot(h, wgm_ref[...])) * _dot(om_ref[...], wom_ref[...])
              + _sigmoid(_dot(h, wgd_ref[...])) * _dot(od_ref[...], wod_ref[...]))
    acc_scr[...] += _dot(merged.astype(BF16), wout_ref[...])

    @pl.when(j == pl.num_programs(1) - 1)
    def _():
        x1 = acc_scr[...]
        x1_ref[...] = x1
        hf_ref[...] = _rms(x1, gffn_ref[...]).astype(BF16)


def _merge_out(h, om, od, xf, w, tm, tn):
    n, d = xf.shape
    row = lambda width: pl.BlockSpec((tm, width), lambda i, j: (i, 0))
    col = lambda rows: pl.BlockSpec((rows, tn), lambda i, j: (0, j))
    return pl.pallas_call(
        _merge_out_kernel,
        grid=(n // tm, d // tn),
        in_specs=[row(d), row(om.shape[1]), row(od.shape[1]), row(d),
                  col(d), col(d), col(om.shape[1]), col(od.shape[1]),
                  pl.BlockSpec((tn, d), lambda i, j: (j, 0)),
                  pl.BlockSpec((1, d), lambda i, j: (0, 0))],
        out_specs=[row(d), row(d)],
        out_shape=[jax.ShapeDtypeStruct((n, d), F32), jax.ShapeDtypeStruct((n, d), BF16)],
        scratch_shapes=[pltpu.VMEM((tm, d), F32)],
        compiler_params=_cparams(("parallel", "arbitrary")),
        name="merge_out",
    )(h, om, od, xf, w['w_gm'], w['w_gd'], w['w_om'], w['w_od'], w['w_out'], w['ffn_norm'])


def _gate_rows_kernel(h_ref, w_ref, o_ref):
    o_ref[...] = _dot(h_ref[...], w_ref[...])


def _gate_rows(rows, w_gate, tf):
    r, d = rows.shape
    return pl.pallas_call(
        _gate_rows_kernel,
        grid=(D_FF // tf,),
        in_specs=[pl.BlockSpec((r, d), lambda j: (0, 0)), pl.BlockSpec((d, tf), lambda j: (0, j))],
        out_specs=pl.BlockSpec((r, tf), lambda j: (0, j)),
        out_shape=jax.ShapeDtypeStruct((r, D_FF), F32),
        compiler_params=_cparams(("parallel",)),
        name="ffn_gate_rows",
    )(rows, w_gate)


def _split3(x):
    hi = x.astype(BF16)
    r1 = x - hi.astype(F32)
    mid = r1.astype(BF16)
    lo = (r1 - mid.astype(F32)).astype(BF16)
    return hi, mid, lo


def _ffn_kernel(hf_ref, x1_ref, halo_ref, wg_ref, wu_ref, wd_ref, cw_ref, cb_ref, y_ref, acc_scr,
                *, tm, t_seg, halo_rows):
    j = pl.program_id(1)

    @pl.when(j == 0)
    def _():
        acc_scr[...] = x1_ref[...]

    hf = hf_ref[...]

    seg_shift = t_seg.bit_length() - 1
    row = lax.broadcasted_iota(jnp.int32, (tm, halo_rows), 0)
    colh = lax.broadcasted_iota(jnp.int32, (tm, halo_rows), 1)
    seg2 = (row >> seg_shift) * 2
    r_in = row & (t_seg - 1)
    e1 = jnp.where((r_in == 0) & (colh == seg2 + 1), 1.0, 0.0).astype(BF16)
    e2 = jnp.where(((r_in == 0) & (colh == seg2)) | ((r_in == 1) & (colh == seg2 + 1)),
                   1.0, 0.0).astype(BF16)
    pieces = _split3(halo_ref[0])
    fix1 = _dot(e1, pieces[0]) + _dot(e1, pieces[1]) + _dot(e1, pieces[2])
    fix2 = _dot(e2, pieces[0]) + _dot(e2, pieces[1]) + _dot(e2, pieces[2])
    r1 = lax.broadcasted_iota(jnp.int32, (tm, 1), 0) & (t_seg - 1)
    cw = cw_ref[...]
    cb = cb_ref[...]

    g = _dot(hf, wg_ref[...])
    up = _dot(hf, wu_ref[...])
    g1 = jnp.where(r1 < 1, fix1, pltpu.roll(g, 1, 0))
    g2 = jnp.where(r1 < 2, fix2, pltpu.roll(g, 2, 0))
    conv = cb + cw[0:1, :] * g2 + cw[1:2, :] * g1 + cw[2:3, :] * g
    act = conv * _sigmoid(conv) * up
    acc_scr[...] += _dot(act.astype(BF16), wd_ref[...])

    @pl.when(j == pl.num_programs(1) - 1)
    def _():
        y_ref[...] = acc_scr[...]


def _ffn(hf, x1, halo, w, tm, tf, t_seg):
    n, d = x1.shape
    halo_rows = halo.shape[1]
    kern = functools.partial(_ffn_kernel, tm=tm, t_seg=t_seg, halo_rows=halo_rows)
    row = lambda width: pl.BlockSpec((tm, width), lambda i, j: (i, 0))
    return pl.pallas_call(
        kern,
        grid=(n // tm, D_FF // tf),
        in_specs=[row(d), row(d),
                  pl.BlockSpec((1, halo_rows, tf), lambda i, j: (i, 0, j)),
                  pl.BlockSpec((d, tf), lambda i, j: (0, j)),
                  pl.BlockSpec((d, tf), lambda i, j: (0, j)),
                  pl.BlockSpec((tf, d), lambda i, j: (j, 0)),
                  pl.BlockSpec((CONV_W, tf), lambda i, j: (0, j)),
                  pl.BlockSpec((1, tf), lambda i, j: (0, j))],
        out_specs=row(d),
        out_shape=jax.ShapeDtypeStruct((n, d), F32),
        scratch_shapes=[pltpu.VMEM((tm, d), F32)],
        compiler_params=_cparams(("parallel", "arbitrary")),
        name="conv_ffn",
    )(hf, x1, halo, w['w_ffn_gate'], w['w_ffn_upv'], w['w_ffn_down'], w['conv_w'], w['conv_b'])


def _pad_cols(a, width):
    return jnp.pad(a, ((0, 0), (0, width - a.shape[1])))


def _prep_weights(attn_norm, w_in, q_a_norm, w_q_up, kv_a_norm, w_kv_up, mla_q_nope_norm,
                  mla_q_rope_norm, mla_k_nope_norm, mla_k_rope_norm, dsa_q_norm, dsa_k_norm,
                  w_o_mla, w_o_dsa, w_out, ffn_norm, w_ffn_up, conv_w, conv_b, w_ffn_down):
    d = w_in.shape[0]
    widths = (Q_LORA, KV_LORA + MLA_ROPE, DSA_HEADS * DSA_HD, 2 * DSA_KV_HEADS * DSA_HD,
              IDX_HEADS * IDX_DIM, IDX_DIM, IDX_HEADS, 2 * d)
    offs = [0]
    for wd in widths:
        offs.append(offs[-1] + wd)
    seg = [w_in[:, offs[i]:offs[i + 1]] for i in range(len(widths))]
    w = {}
    w['w_cq'] = seg[0].astype(BF16)
    w['w_sm'] = jnp.concatenate(
        [seg[1][:, :KV_LORA], _pad_cols(seg[1][:, KV_LORA:], LANES), _pad_cols(seg[5], LANES)],
        axis=1).astype(BF16)
    w['w_wit'] = seg[6].T.astype(BF16)
    w['w_qd'] = seg[2].astype(BF16)
    w['w_kvd'] = seg[3].astype(BF16)
    w['w_qi'] = jnp.pad(seg[4].reshape(d, IDX_HEADS, IDX_DIM),
                        ((0, 0), (0, 0), (0, LANES - IDX_DIM))).reshape(d, IDX_HEADS * LANES).astype(BF16)
    w['w_gm'] = seg[7][:, :d].astype(BF16)
    w['w_gd'] = seg[7][:, d:].astype(BF16)
    qup = w_q_up.reshape(Q_LORA, MLA_HEADS, MLA_NOPE + MLA_ROPE)
    w['w_qup'] = jnp.pad(qup, ((0, 0), (0, 0), (0, MLA_QK_PAD - MLA_NOPE - MLA_ROPE))
                         ).reshape(Q_LORA, MLA_HEADS * MLA_QK_PAD).astype(BF16)
    kvup = w_kv_up.reshape(KV_LORA, MLA_HEADS, MLA_NOPE + MLA_V)
    w['w_kvup'] = jnp.concatenate(
        [kvup[:, :, :MLA_NOPE].reshape(KV_LORA, MLA_HEADS * MLA_NOPE),
         kvup[:, :, MLA_NOPE:].reshape(KV_LORA, MLA_HEADS * MLA_V)], axis=1).astype(BF16)
    r2 = lambda a: a.reshape(1, -1).astype(F32)
    w['attn_norm'] = r2(attn_norm)
    w['q_a_norm'] = r2(q_a_norm)
    w['kv_a_norm'] = r2(kv_a_norm)
    w['g_kpe'] = _pad_cols(r2(mla_k_rope_norm), LANES)
    w['g_qnope'] = r2(mla_q_nope_norm)
    w['g_qrope'] = _pad_cols(r2(mla_q_rope_norm), LANES)
    w['g_knope'] = r2(mla_k_nope_norm)
    w['dsa_q_norm'] = r2(dsa_q_norm)
    w['dsa_k_norm'] = r2(dsa_k_norm)
    w['ffn_norm'] = r2(ffn_norm)
    w['w_om'] = w_o_mla.astype(BF16)
    w['w_od'] = w_o_dsa.astype(BF16)
    w['w_out'] = w_out.astype(BF16)
    w['w_ffn_gate'] = w_ffn_up[:, :D_FF].astype(BF16)
    w['w_ffn_upv'] = w_ffn_up[:, D_FF:].astype(BF16)
    w['w_ffn_down'] = w_ffn_down.astype(BF16)
    w['conv_w'] = conv_w.astype(F32)
    w['conv_b'] = r2(conv_b)
    return w


def _rope_tables(pos, rot):
    half = rot // 2
    inv = ROPE_THETA ** (-jnp.arange(half, dtype=F32) / half)
    ang = pos.astype(F32)[:, None] * inv[None, :]
    cos, sin = jnp.cos(ang), jnp.sin(ang)
    t = pos.shape[0]
    c = jnp.concatenate([cos, cos, jnp.ones((t, LANES - rot), F32)], axis=1)
    s = jnp.concatenate([-sin, sin, jnp.zeros((t, LANES - rot), F32)], axis=1)
    return c, s


def _round_up(a, m):
    return (a + m - 1) // m * m


def _tile(n, target):
    t = min(n, target)
    assert n % t == 0, (n, t)
    return t


def _layer(x, pos0, past, w):
    b, t_len, d = x.shape
    n = b * t_len
    xf = x.reshape(n, d)

    tm = _tile(n, 512)
    tm_in = _tile(n, 256)
    assert tm % t_len == 0 or t_len % tm == 0
    assert tm_in % t_len == 0 or t_len % tm_in == 0
    pos = pos0 + jnp.arange(t_len, dtype=jnp.int32)
    tabs = []
    for rot in (MLA_ROPE, DSA_ROT, IDX_ROT):
        c, s = _rope_tables(pos, rot)
        if t_len < tm_in:
            c, s = jnp.tile(c, (tm_in // t_len, 1)), jnp.tile(s, (tm_in // t_len, 1))
        tabs += [c, s]

    (h, qfull, ckv, kpe, kpeb, qd, knew, vnew, knewb, vnewb, qi, ki, kib, wit) = _in_proj(
        xf, tabs, w, tm_in, t_len)

    if past is None:
        s_len = t_len
        tk = _tile(s_len, 512)
        s_pad = s_len
        all_ckv, all_kpeb, all_kb, all_vb, all_kib = ckv, kpeb, knewb, vnewb, kib
        conv_hist = jnp.zeros((b, CONV_W - 1, D_FF), F32)
    else:
        p_ckv, p_kpe, p_k, p_v, p_ki, conv_hist = past
        past_len = p_ckv.shape[1]
        s_len = past_len + t_len
        tk = 384 if s_len > 384 else _round_up(s_len, LANES)
        s_pad = _round_up(s_len, tk)
        padr = s_pad - s_len

        def cat(p_arr, new, width, dtype, lane_pad=0):
            p2 = p_arr.reshape(b, past_len, width).astype(dtype)
            if lane_pad:
                p2 = jnp.pad(p2, ((0, 0), (0, 0), (0, lane_pad)))
            a = jnp.concatenate([p2, new.reshape(b, t_len, width + lane_pad)], axis=1)
            return jnp.pad(a, ((0, 0), (0, padr), (0, 0))).reshape(b * s_pad, width + lane_pad)

        all_ckv = cat(p_ckv, ckv, KV_LORA, F32)
        all_kpeb = cat(p_kpe, kpeb, MLA_ROPE, BF16, LANES - MLA_ROPE)
        all_kb = cat(p_k, knewb, DSA_KV_HEADS * DSA_HD, BF16)
        all_vb = cat(p_v, vnewb, DSA_KV_HEADS * DSA_HD, BF16)
        all_kib = cat(p_ki, kib, IDX_DIM, BF16, LANES - IDX_DIM)

    kfull, vt_mla = _kv_up(all_ckv, all_kpeb, w, tk)
    tq = _tile(t_len, 256)
    n_ch = s_pad // tk
    vt_dsa = all_vb.reshape(b * n_ch, tk, DSA_KV_HEADS, DSA_HD).transpose(0, 2, 3, 1)
    o_mla = _mla_attn(qfull, kfull, vt_mla, b, t_len, s_pad, tq, tk, pos0, s_len, 4)
    topk = min(TOPK_MAX, s_len // 4)
    o_dsa = _dsa_attn(qi, wit, qd, all_kib, all_kb, vt_dsa, b, t_len, s_pad, tq, tk, pos0, s_len,
                      topk)

    x1, hf = _merge_out(h, o_mla, o_dsa, xf, w, tm, 512)

    t_seg = min(t_len, tm)
    assert t_seg & (t_seg - 1) == 0 and t_seg >= CONV_W - 1
    n_t = t_len // t_seg
    segs = tm // t_seg
    brows = hf.reshape(b, n_t, t_seg, d)[:, :, t_seg - (CONV_W - 1):, :].reshape(-1, d)
    bg = _gate_rows(brows, w['w_ffn_gate'], 512).reshape(b, n_t, CONV_W - 1, D_FF)
    halo = jnp.concatenate([conv_hist.astype(F32)[:, None], bg[:, :-1]], axis=1)
    halo = halo.reshape(n // tm, segs * (CONV_W - 1), D_FF)
    halo_rows = _round_up(halo.shape[1], 16)
    halo = jnp.pad(halo, ((0, 0), (0, halo_rows - halo.shape[1]), (0, 0)))
    y = _ffn(hf, x1, halo, w, tm, 512, t_seg)

    state = (ckv.reshape(b, t_len, KV_LORA), kpe.reshape(b, t_len, MLA_ROPE),
             knew.reshape(b, t_len, DSA_KV_HEADS, DSA_HD), vnew.reshape(b, t_len, DSA_KV_HEADS, DSA_HD),
             ki.reshape(b, t_len, IDX_DIM), bg[:, -1])
    return y.reshape(b, t_len, d), state


def kernel(x_prompt, x_sample, cache_mla_ckv, cache_mla_kpe, cache_dsa_k, cache_dsa_v, cache_idx_k, state_ffn_conv, attn_norm, w_in, q_a_norm, w_q_up, kv_a_norm, w_kv_up, mla_q_nope_norm, mla_q_rope_norm, mla_k_nope_norm, mla_k_rope_norm, dsa_q_norm, dsa_k_norm, w_o_mla, w_o_dsa, w_out, ffn_norm, w_ffn_up, conv_w, conv_b, w_ffn_down):
    depth = w_in.shape[0]
    past_len = cache_mla_ckv.shape[2]
    xp, xs = x_prompt, x_sample
    p_states, s_states = [], []
    for l in range(depth):
        w = _prep_weights(attn_norm[l], w_in[l], q_a_norm[l], w_q_up[l], kv_a_norm[l], w_kv_up[l],
                          mla_q_nope_norm[l], mla_q_rope_norm[l], mla_k_nope_norm[l],
                          mla_k_rope_norm[l], dsa_q_norm[l], dsa_k_norm[l], w_o_mla[l], w_o_dsa[l],
                          w_out[l], ffn_norm[l], w_ffn_up[l], conv_w[l], conv_b[l], w_ffn_down[l])
        past = (cache_mla_ckv[l], cache_mla_kpe[l], cache_dsa_k[l], cache_dsa_v[l],
                cache_idx_k[l], state_ffn_conv[l])
        xp, sp = _layer(xp, 0, None, w)
        xs, ss = _layer(xs, past_len, past, w)
        p_states.append(sp)
        s_states.append(ss)
    p_out = [jnp.stack(a) for a in zip(*p_states)]
    s_out = [jnp.stack(a) for a in zip(*s_states)]
    return (xp, xs, *p_out, *s_out)
```

```python
import functools

import jax
import jax.numpy as jnp
from jax import lax
from jax.experimental import pallas as pl
from jax.experimental.pallas import tpu as pltpu

CHUNK = 64
CHUNK_SHIFT = 6
ROPE_THETA = 500000.0
NORM_EPS = 1e-6
NEG_INF = -1e30
LOG2E = 1.4426950408889634
MLA_HEADS = 8
Q_LORA = 512
KV_LORA = 256
MLA_NOPE = 128
MLA_ROPE = 64
MLA_V = 128
MLA_QK_PAD = 256
DSA_HEADS = 8
DSA_KV_HEADS = 2
DSA_HD = 128
DSA_ROT = DSA_HD // 4
IDX_HEADS = 16
IDX_DIM = 64
IDX_ROT = IDX_DIM // 4
TOPK_MAX = 256
D_FF = 5632
CONV_W = 3
MLA_LOOKAHEAD = 6
DSA_LOOKAHEAD = 4
CHUNK_UNROLL = 2
COUNT_ROWS = 32

LANES = 128
SUBLANES = 8
VMEM_LIMIT = 56 * 1024 * 1024
INT_MIN = -2 ** 31

F32 = jnp.float32
BF16 = jnp.bfloat16


def _cparams(sem, flags=None):
    return pltpu.CompilerParams(dimension_semantics=sem, vmem_limit_bytes=VMEM_LIMIT, flags=flags)


def _resident(shape):
    nd = len(shape)
    return pl.BlockSpec(shape, lambda *_: (0,) * nd, pipeline_mode=pl.Buffered(1))


def _rms(x, g, n=None):
    n = x.shape[-1] if n is None else n
    ms = jnp.sum(x * x, axis=-1, keepdims=True) * (1.0 / n)
    return x * lax.rsqrt(ms + NORM_EPS) * g


def _rope128(x, c, s, half):
    lane = lax.broadcasted_iota(jnp.int32, x.shape, 1)
    swapped = jnp.where(lane < half, pltpu.roll(x, LANES - half, 1), pltpu.roll(x, half, 1))
    return x * c + swapped * s


def _dot(a, b):
    return jnp.dot(a, b, preferred_element_type=F32)


def _dot_t(a, b):
    return lax.dot_general(a, b, (((1,), (1,)), ((), ())), preferred_element_type=F32)


def _in_proj_kernel(x_ref, gattn_ref, wcq_ref, wsm_ref, wqd_ref, wkvd_ref, wqi_ref, wqup_ref, wwit_ref,
                    gqa_ref, gkva_ref, gkpe_ref, gqn_ref, gqr_ref, gdq_ref, gdk_ref,
                    c64_ref, s64_ref, c32_ref, s32_ref, c16_ref, s16_ref,
                    h_ref, qfull_ref, ckv_ref, kpe_ref, kpeb_ref, qd_ref, knew_ref, vnew_ref,
                    knewb_ref, vnewb_ref, qi_ref, ki_ref, kib_ref, wit_ref):
    x = x_ref[...]
    hb = _rms(x, gattn_ref[...]).astype(BF16)
    h_ref[...] = hb
    c64, s64 = c64_ref[...], s64_ref[...]
    c32, s32 = c32_ref[...], s32_ref[...]
    c16, s16 = c16_ref[...], s16_ref[...]

    cqn = _rms(_dot(hb, wcq_ref[...]), gqa_ref[...]).astype(BF16)
    q = _dot(cqn, wqup_ref[...])
    scale = (MLA_NOPE + MLA_ROPE) ** -0.5 * LOG2E
    for hd in range(MLA_HEADS):
        lo = hd * MLA_QK_PAD
        nope = _rms(q[:, lo:lo + LANES], gqn_ref[...]) * scale
        qfull_ref[:, lo:lo + LANES] = nope.astype(BF16)
        pe = _rms(q[:, lo + LANES:lo + 2 * LANES], gqr_ref[...], MLA_ROPE)
        pe = _rope128(pe, c64, s64, MLA_ROPE // 2) * scale
        qfull_ref[:, lo + LANES:lo + 2 * LANES] = pe.astype(BF16)

    sm = _dot(hb, wsm_ref[...])
    ckv_ref[...] = _rms(sm[:, :KV_LORA], gkva_ref[...])
    kpe = _rope128(_rms(sm[:, KV_LORA:KV_LORA + LANES], gkpe_ref[...], MLA_ROPE), c64, s64,
                   MLA_ROPE // 2)
    kpe_ref[...] = kpe[:, :MLA_ROPE]
    kpeb_ref[...] = kpe.astype(BF16)
    ki = _rope128(sm[:, KV_LORA + LANES:KV_LORA + 2 * LANES], c16, s16, IDX_ROT // 2)
    ki_ref[...] = ki[:, :IDX_DIM]
    kib_ref[...] = ki.astype(BF16)
    wit_ref[...] = _dot_t(wwit_ref[...], hb) * (IDX_HEADS ** -0.5)

    qd = _dot(hb, wqd_ref[...])
    for hd in range(DSA_HEADS):
        lo = hd * DSA_HD
        t = _rope128(_rms(qd[:, lo:lo + DSA_HD], gdq_ref[...]), c32, s32, DSA_ROT // 2)
        qd_ref[:, lo:lo + DSA_HD] = (t * (DSA_HD ** -0.5 * LOG2E)).astype(BF16)
    kvd = _dot(hb, wkvd_ref[...])
    for hd in range(DSA_KV_HEADS):
        lo = hd * DSA_HD
        t = _rope128(_rms(kvd[:, lo:lo + DSA_HD], gdk_ref[...]), c32, s32, DSA_ROT // 2)
        knew_ref[:, lo:lo + DSA_HD] = t
        knewb_ref[:, lo:lo + DSA_HD] = t.astype(BF16)
    v = kvd[:, DSA_KV_HEADS * DSA_HD:]
    vnew_ref[...] = v
    vnewb_ref[...] = v.astype(BF16)

    qi = _dot(hb, wqi_ref[...])
    for hd in range(IDX_HEADS):
        lo = hd * LANES
        t = _rope128(qi[:, lo:lo + LANES], c16, s16, IDX_ROT // 2)
        qi_ref[:, lo:lo + LANES] = (t * (IDX_DIM ** -0.5)).astype(BF16)


def _in_proj(xf, tabs, w, tm, t_len):
    n, d = xf.shape
    n_tiles = n // tm
    if t_len >= tm:
        per = t_len // tm
        tab_map = lambda i: (i % per, 0)
    else:
        tab_map = lambda i: (0, 0)
    row = lambda width: pl.BlockSpec((tm, width), lambda i: (i, 0))
    tab = pl.BlockSpec((tm, LANES), tab_map)
    weights = [w['attn_norm'], w['w_cq'], w['w_sm'], w['w_qd'], w['w_kvd'], w['w_qi'], w['w_qup'],
               w['w_wit'],
               w['q_a_norm'], w['kv_a_norm'], w['g_kpe'], w['g_qnope'], w['g_qrope'],
               w['dsa_q_norm'], w['dsa_k_norm']]
    out_widths = [(d, BF16), (MLA_HEADS * MLA_QK_PAD, BF16), (KV_LORA, F32), (MLA_ROPE, F32),
                  (LANES, BF16), (DSA_HEADS * DSA_HD, BF16), (DSA_KV_HEADS * DSA_HD, F32),
                  (DSA_KV_HEADS * DSA_HD, F32), (DSA_KV_HEADS * DSA_HD, BF16),
                  (DSA_KV_HEADS * DSA_HD, BF16), (IDX_HEADS * LANES, BF16), (IDX_DIM, F32),
                  (LANES, BF16)]
    return pl.pallas_call(
        _in_proj_kernel,
        grid=(n_tiles,),
        in_specs=[row(d)] + [_resident(a.shape) for a in weights] + [tab] * 6,
        out_specs=[row(wd) for wd, _ in out_widths]
        + [pl.BlockSpec((IDX_HEADS, tm), lambda i: (0, i))],
        out_shape=[jax.ShapeDtypeStruct((n, wd), dt) for wd, dt in out_widths]
        + [jax.ShapeDtypeStruct((IDX_HEADS, n), F32)],
        compiler_params=_cparams(("parallel",)),
        name="in_proj",
    )(xf, *weights, *tabs)


def _kv_up_kernel(ckv_ref, kpe_ref, w_ref, g_ref, kfull_ref, vt_ref):
    kv = _dot(ckv_ref[...].astype(BF16), w_ref[...])
    kpe = kpe_ref[...]
    g = g_ref[...]
    for hd in range(MLA_HEADS):
        lo = hd * MLA_QK_PAD
        kfull_ref[:, lo:lo + LANES] = _rms(kv[:, hd * LANES:(hd + 1) * LANES], g).astype(BF16)
        kfull_ref[:, lo + LANES:lo + 2 * LANES] = kpe
    for hd in range(MLA_HEADS):
        lo = MLA_HEADS * MLA_NOPE + hd * MLA_V
        vt_ref[0, hd] = kv[:, lo:lo + MLA_V].T.astype(BF16)


def _kv_up(ckv, kpeb, w, tm):
    n = ckv.shape[0]
    row = lambda width: pl.BlockSpec((tm, width), lambda i: (i, 0))
    return pl.pallas_call(
        _kv_up_kernel,
        grid=(n // tm,),
        in_specs=[row(KV_LORA), row(LANES), _resident(w['w_kvup'].shape),
                  _resident(w['g_knope'].shape)],
        out_specs=[row(MLA_HEADS * MLA_QK_PAD),
                   pl.BlockSpec((1, MLA_HEADS, MLA_V, tm), lambda i: (i, 0, 0, 0))],
        out_shape=[jax.ShapeDtypeStruct((n, MLA_HEADS * MLA_QK_PAD), BF16),
                   jax.ShapeDtypeStruct((n // tm, MLA_HEADS, MLA_V, tm), BF16)],
        compiler_params=_cparams(("parallel",)),
        name="kv_up",
    )(ckv, kpeb, w['w_kvup'], w['g_knope'])


def _visible_limit(q_last_pos, s_valid):
    return jnp.minimum(((q_last_pos >> CHUNK_SHIFT) + 1) * CHUNK, s_valid)


def _softmax_step(s, vt, m, l, acc):
    m_new = jnp.maximum(m, jnp.max(s, axis=0, keepdims=True))
    alpha = jnp.exp2(m - m_new)
    p = jnp.exp2(s - m_new)
    l = alpha * l + jnp.sum(p, axis=0, keepdims=True)
    acc = alpha * acc + _dot(vt, p.astype(BF16))
    return m_new, l, acc


def _run_tiles(tiles, score_fn, value_fn, carry, lookahead):
    carry = list(carry)
    pending = [score_fn(*t) for t in tiles[:lookahead]]
    for i, (stream, chunk) in enumerate(tiles):
        s = pending.pop(0)
        carry[stream] = _softmax_step(s, value_fn(stream, chunk), *carry[stream])
        if i + lookahead < len(tiles):
            pending.append(score_fn(*tiles[i + lookahead]))
    return tuple(carry)


def _softmax_init(tq, dv, streams):
    return tuple((jnp.full((1, tq), NEG_INF, F32), jnp.zeros((1, tq), F32),
                  jnp.zeros((dv, tq), F32)) for _ in range(streams))


def _lane_width(tq):
    return max(tq, LANES)


def _transposed(block):
    x = block.astype(F32)
    tq, d = x.shape
    if tq < _lane_width(tq):
        x = jnp.concatenate([x, jnp.zeros((_lane_width(tq) - tq, d), F32)], axis=0)
    return x.T.astype(BF16)


def _visibility(start, tk, qchunk, s_valid):
    kpos = start + lax.broadcasted_iota(jnp.int32, (tk, 1), 0)
    return ((kpos >> CHUNK_SHIFT) <= qchunk) & (kpos < s_valid)


def _mla_kernel(q_ref, k_ref, vt_ref, o_ref, *, tq, tk, pos0, s_valid, nh):
    qt = pl.program_id(2)
    q_first = pos0 + qt * tq
    tql = _lane_width(tq)
    qchunk = (q_first + lax.broadcasted_iota(jnp.int32, (1, tql), 1)) >> CHUNK_SHIFT
    n_chunks = pl.cdiv(_visible_limit(q_first + tq - 1, s_valid), tk)
    n_full = _visible_limit(q_first, s_valid) // tk
    qts = [_transposed(q_ref[:, hd * MLA_QK_PAD:(hd + 1) * MLA_QK_PAD]) for hd in range(nh)]

    def make_body(masked, unroll):
        def body(cu, carry):
            def score(hd, u):
                start = pl.multiple_of((cu * unroll + u) * tk, tk)
                s = _dot(k_ref[pl.ds(start, tk), hd * MLA_QK_PAD:(hd + 1) * MLA_QK_PAD], qts[hd])
                if masked:
                    s = jnp.where(_visibility(start, tk, qchunk, s_valid), s, NEG_INF)
                return s

            def value(hd, u):
                return vt_ref[cu * unroll + u, hd]

            tiles = [(hd, u) for u in range(unroll) for hd in range(nh)]
            return _run_tiles(tiles, score, value, carry, MLA_LOOKAHEAD)
        return body

    n_pairs = n_full // CHUNK_UNROLL
    carry = lax.fori_loop(0, n_pairs, make_body(False, CHUNK_UNROLL),
                          _softmax_init(tql, MLA_V, nh))
    carry = lax.fori_loop(n_pairs * CHUNK_UNROLL, n_chunks, make_body(True, 1), carry)
    for hd in range(nh):
        _, l, acc = carry[hd]
        o_ref[:, hd * MLA_V:(hd + 1) * MLA_V] = (acc / l).T[:tq].astype(BF16)


def _mla_attn(qfull, kfull, vt, b, t_len, s_pad, tq, tk, pos0, s_valid, nh):
    nq = t_len // tq
    n_ch = s_pad // tk
    kern = functools.partial(_mla_kernel, tq=tq, tk=tk, pos0=pos0, s_valid=s_valid, nh=nh)
    return pl.pallas_call(
        kern,
        grid=(b, MLA_HEADS // nh, nq),
        in_specs=[pl.BlockSpec((tq, nh * MLA_QK_PAD), lambda bi, h, i: (bi * nq + i, h)),
                  pl.BlockSpec((s_pad, nh * MLA_QK_PAD), lambda bi, h, i: (bi, h)),
                  pl.BlockSpec((n_ch, nh, MLA_V, tk), lambda bi, h, i: (bi, h, 0, 0))],
        out_specs=pl.BlockSpec((tq, nh * MLA_V), lambda bi, h, i: (bi * nq + i, h)),
        out_shape=jax.ShapeDtypeStruct((b * t_len, MLA_HEADS * MLA_V), BF16),
        compiler_params=_cparams(("parallel", "parallel", "arbitrary")),
        name="mla_attn",
    )(qfull, kfull, vt)


def _count_keys(preds, key_scr, n_chunks, tq, tk):
    def make_body(unroll):
        def body(cu, parts):
            parts = list(parts)
            for u in range(unroll):
                c = cu * unroll + u
                keys = key_scr[c]
                idx = c * tk + lax.broadcasted_iota(jnp.int32, (tk, 1), 0)
                for n, pred in enumerate(preds):
                    hit = jnp.where(pred(keys, idx), 1, 0).reshape(tk // COUNT_ROWS, COUNT_ROWS, tq)
                    parts[n] = parts[n] + jnp.sum(hit, axis=0)
            return tuple(parts)
        return body
    zero = jnp.zeros((COUNT_ROWS, tq), jnp.int32)
    n_pairs = n_chunks // CHUNK_UNROLL
    parts = lax.fori_loop(0, n_pairs, make_body(CHUNK_UNROLL), (zero,) * len(preds))
    parts = lax.fori_loop(n_pairs * CHUNK_UNROLL, n_chunks, make_body(1), parts)
    return [jnp.sum(p, axis=0, keepdims=True) for p in parts]


def _dsa_kernel(qi_ref, wit_ref, qd_ref, ki_ref, k_ref, vt_ref, o_ref, key_scr, bias_scr, cut_scr,
                *, tq, tk, streams, pos0, s_valid, topk, idx_bits):
    qt = pl.program_id(1)
    q_first = pos0 + qt * tq
    tql = _lane_width(tq)
    lane = lax.broadcasted_iota(jnp.int32, (1, tql), 1)
    qchunk = (q_first + lane) >> CHUNK_SHIFT
    n_chunks = pl.cdiv(_visible_limit(q_first + tq - 1, s_valid), tk)
    wit = wit_ref[0]
    if tq < tql:
        wit = jnp.concatenate([wit, jnp.zeros((IDX_HEADS, tql - tq), F32)], axis=1)
    qits = [_transposed(qi_ref[:, hd * LANES:(hd + 1) * LANES]) for hd in range(IDX_HEADS)]

    def score_body(c, _):
        start = pl.multiple_of(c * tk, tk)
        kic = ki_ref[pl.ds(start, tk), :]
        sc = jnp.zeros((tk, tql), F32)
        for hd in range(IDX_HEADS):
            sc = sc + wit[hd:hd + 1, :] * jnp.maximum(_dot(kic, qits[hd]), 0.0)
        bits = pltpu.bitcast(sc + 0.0, jnp.int32)
        keys = bits ^ ((bits >> 31) & 0x7FFFFFFF)
        key_scr[c] = jnp.where(_visibility(start, tk, qchunk, s_valid), keys, INT_MIN)
        return 0
    lax.fori_loop(0, n_chunks, score_body, 0)

    def bit_body(b, t):
        cand = t + (jnp.int32(1) << (31 - b))
        cnt, = _count_keys([lambda kk, _: kk >= cand], key_scr, n_chunks, tql, tk)
        return jnp.where(cnt >= topk, cand, t)
    thr = lax.fori_loop(0, 32, bit_body, jnp.full((1, tql), INT_MIN, jnp.int32))

    cnt_gt, cnt_eq = _count_keys([lambda kk, _: kk > thr, lambda kk, _: kk == thr],
                                 key_scr, n_chunks, tql, tk)
    need = topk - cnt_gt
    cut_scr[...] = jnp.full((SUBLANES, tql), 2 ** 30, jnp.int32)
    excess = jnp.max(jnp.where((cnt_eq > need) & (thr > INT_MIN) & (lane < tq), 1, 0))

    @pl.when(excess > 0)
    def _():
        def idx_body(b, pos):
            cand = pos + (jnp.int32(1) << (idx_bits - 1 - b))
            cnt, = _count_keys([lambda kk, idx: (kk == thr) & (idx < cand)], key_scr, n_chunks,
                               tql, tk)
            return jnp.where(cnt < need, cand, pos)
        pos = lax.fori_loop(0, idx_bits, idx_body, jnp.zeros((1, tql), jnp.int32))
        cut_scr[...] = jnp.broadcast_to(pos, (SUBLANES, tql))
    cut = jnp.where(thr > INT_MIN, cut_scr[0:1, :], -1)

    def bias_body(c, _):
        keys = key_scr[c]
        idx = c * tk + lax.broadcasted_iota(jnp.int32, (tk, 1), 0)
        tie = jnp.where(idx <= cut, 0.0, NEG_INF)
        bias_scr[c] = jnp.where(keys > thr, 0.0, jnp.where(keys == thr, tie, NEG_INF))
        return 0
    lax.fori_loop(0, n_chunks, bias_body, 0)

    rep = DSA_HEADS // DSA_KV_HEADS
    for h0 in range(0, DSA_HEADS, streams):
        qts = [_transposed(qd_ref[:, hd * DSA_HD:(hd + 1) * DSA_HD]) for hd in range(h0, h0 + streams)]

        def make_att_body(unroll, qts=qts, h0=h0):
            def att_body(cu, carry):
                def score(n, u):
                    c = cu * unroll + u
                    start = pl.multiple_of(c * tk, tk)
                    g = (h0 + n) // rep
                    kc = k_ref[pl.ds(start, tk), g * DSA_HD:(g + 1) * DSA_HD]
                    return _dot(kc, qts[n]) + bias_scr[c]

                def value(n, u):
                    return vt_ref[cu * unroll + u, (h0 + n) // rep]

                tiles = [(n, u) for u in range(unroll) for n in range(streams)]
                return _run_tiles(tiles, score, value, carry, DSA_LOOKAHEAD)
            return att_body

        n_pairs = n_chunks // CHUNK_UNROLL
        carry = lax.fori_loop(0, n_pairs, make_att_body(CHUNK_UNROLL),
                              _softmax_init(tql, DSA_HD, streams))
        carry = lax.fori_loop(n_pairs * CHUNK_UNROLL, n_chunks, make_att_body(1), carry)
        for n, (_, l, acc) in enumerate(carry):
            hd = h0 + n
            o_ref[:, hd * DSA_HD:(hd + 1) * DSA_HD] = (acc / l).T[:tq].astype(BF16)


def _dsa_attn(qi, wit, qd, kib, kb, vt, b, t_len, s_pad, tq, tk, pos0, s_valid, topk):
    nq = t_len // tq
    n_ch = s_pad // tk
    idx_bits = max(1, (s_pad - 1).bit_length())
    tql = _lane_width(tq)
    wit = wit.reshape(IDX_HEADS, b * nq, tq).transpose(1, 0, 2)
    kern = functools.partial(_dsa_kernel, tq=tq, tk=tk, streams=8, pos0=pos0, s_valid=s_valid,
                             topk=topk, idx_bits=idx_bits)
    qrow = lambda width: pl.BlockSpec((tq, width), lambda bi, i: (bi * nq + i, 0))
    krow = lambda width: pl.BlockSpec((s_pad, width), lambda bi, i: (bi, 0))
    return pl.pallas_call(
        kern,
        grid=(b, nq),
        in_specs=[qrow(IDX_HEADS * LANES),
                  pl.BlockSpec((1, IDX_HEADS, tq), lambda bi, i: (bi * nq + i, 0, 0)),
                  qrow(DSA_HEADS * DSA_HD), krow(LANES), krow(DSA_KV_HEADS * DSA_HD),
                  pl.BlockSpec((n_ch, DSA_KV_HEADS, DSA_HD, tk), lambda bi, i: (bi, 0, 0, 0))],
        out_specs=qrow(DSA_HEADS * DSA_HD),
        out_shape=jax.ShapeDtypeStruct((b * t_len, DSA_HEADS * DSA_HD), BF16),
        scratch_shapes=[pltpu.VMEM((n_ch, tk, tql), jnp.int32), pltpu.VMEM((n_ch, tk, tql), F32),
                        pltpu.VMEM((SUBLANES, tql), jnp.int32)],
        compiler_params=_cparams(("parallel", "arbitrary")),
        name="dsa_attn",
    )(qi, wit, qd, kib, kb, vt)


def _sigmoid(x):
    return 1.0 / (1.0 + jnp.exp(-x))


def _merge_out_kernel(h_ref, om_ref, od_ref, x_ref, wgm_ref, wgd_ref, wom_ref, wod_ref, wout_ref,
                      gffn_ref, x1_ref, hf_ref, acc_scr):
    j = pl.program_id(1)

    @pl.when(j == 0)
    def _():
        acc_scr[...] = x_ref[...]

    h = h_ref[...]
    merged = (_sigmoid(_dot(h, wgm_ref[...])) * _dot(om_ref[...], wom_ref[...])
              + _sigmoid(_dot(h, wgd_ref[...])) * _dot(od_ref[...], wod_ref[...]))
    acc_scr[...] += _dot(merged.astype(BF16), wout_ref[...])

    @pl.when(j == pl.num_programs(1) - 1)
    def _():
        x1 = acc_scr[...]
        x1_ref[...] = x1
        hf_ref[...] = _rms(x1, gffn_ref[...]).astype(BF16)


def _merge_out(h, om, od, xf, w, tm, tn):
    n, d = xf.shape
    row = lambda width: pl.BlockSpec((tm, width), lambda i, j: (i, 0))
    col = lambda rows: pl.BlockSpec((rows, tn), lambda i, j: (0, j))
    return pl.pallas_call(
        _merge_out_kernel,
        grid=(n // tm, d // tn),
        in_specs=[row(d), row(om.shape[1]), row(od.shape[1]), row(d),
                  col(d), col(d), col(om.shape[1]), col(od.shape[1]),
                  pl.BlockSpec((tn, d), lambda i, j: (j, 0)),
                  pl.BlockSpec((1, d), lambda i, j: (0, 0))],
        out_specs=[row(d), row(d)],
        out_shape=[jax.ShapeDtypeStruct((n, d), F32), jax.ShapeDtypeStruct((n, d), BF16)],
        scratch_shapes=[pltpu.VMEM((tm, d), F32)],
        compiler_params=_cparams(("parallel", "arbitrary")),
        name="merge_out",
    )(h, om, od, xf, w['w_gm'], w['w_gd'], w['w_om'], w['w_od'], w['w_out'], w['ffn_norm'])


def _gate_rows_kernel(h_ref, w_ref, o_ref):
    o_ref[...] = _dot(h_ref[...], w_ref[...])


def _gate_rows(rows, w_gate, tf):
    r, d = rows.shape
    return pl.pallas_call(
        _gate_rows_kernel,
        grid=(D_FF // tf,),
        in_specs=[pl.BlockSpec((r, d), lambda j: (0, 0)), pl.BlockSpec((d, tf), lambda j: (0, j))],
        out_specs=pl.BlockSpec((r, tf), lambda j: (0, j)),
        out_shape=jax.ShapeDtypeStruct((r, D_FF), F32),
        compiler_params=_cparams(("parallel",)),
        name="ffn_gate_rows",
    )(rows, w_gate)


def _split3(x):
    hi = x.astype(BF16)
    r1 = x - hi.astype(F32)
    mid = r1.astype(BF16)
    lo = (r1 - mid.astype(F32)).astype(BF16)
    return hi, mid, lo


def _ffn_kernel(hf_ref, x1_ref, halo_ref, wg_ref, wu_ref, wd_ref, cw_ref, cb_ref, y_ref, acc_scr,
                *, tm, t_seg, halo_rows):
    j = pl.program_id(1)

    @pl.when(j == 0)
    def _():
        acc_scr[...] = x1_ref[...]

    hf = hf_ref[...]

    seg_shift = t_seg.bit_length() - 1
    row = lax.broadcasted_iota(jnp.int32, (tm, halo_rows), 0)
    colh = lax.broadcasted_iota(jnp.int32, (tm, halo_rows), 1)
    seg2 = (row >> seg_shift) * 2
    r_in = row & (t_seg - 1)
    e1 = jnp.where((r_in == 0) & (colh == seg2 + 1), 1.0, 0.0).astype(BF16)
    e2 = jnp.where(((r_in == 0) & (colh == seg2)) | ((r_in == 1) & (colh == seg2 + 1)),
                   1.0, 0.0).astype(BF16)
    pieces = _split3(halo_ref[0])
    fix1 = _dot(e1, pieces[0]) + _dot(e1, pieces[1]) + _dot(e1, pieces[2])
    fix2 = _dot(e2, pieces[0]) + _dot(e2, pieces[1]) + _dot(e2, pieces[2])
    r1 = lax.broadcasted_iota(jnp.int32, (tm, 1), 0) & (t_seg - 1)
    cw = cw_ref[...]
    cb = cb_ref[...]

    g = _dot(hf, wg_ref[...])
    up = _dot(hf, wu_ref[...])
    g1 = jnp.where(r1 < 1, fix1, pltpu.roll(g, 1, 0))
    g2 = jnp.where(r1 < 2, fix2, pltpu.roll(g, 2, 0))
    conv = cb + cw[0:1, :] * g2 + cw[1:2, :] * g1 + cw[2:3, :] * g
    act = conv * _sigmoid(conv) * up
    acc_scr[...] += _dot(act.astype(BF16), wd_ref[...])

    @pl.when(j == pl.num_programs(1) - 1)
    def _():
        y_ref[...] = acc_scr[...]


def _ffn(hf, x1, halo, w, tm, tf, t_seg):
    n, d = x1.shape
    halo_rows = halo.shape[1]
    kern = functools.partial(_ffn_kernel, tm=tm, t_seg=t_seg, halo_rows=halo_rows)
    row = lambda width: pl.BlockSpec((tm, width), lambda i, j: (i, 0))
    return pl.pallas_call(
        kern,
        grid=(n // tm, D_FF // tf),
        in_specs=[row(d), row(d),
                  pl.BlockSpec((1, halo_rows, tf), lambda i, j: (i, 0, j)),
                  pl.BlockSpec((d, tf), lambda i, j: (0, j)),
                  pl.BlockSpec((d, tf), lambda i, j: (0, j)),
                  pl.BlockSpec((tf, d), lambda i, j: (j, 0)),
                  pl.BlockSpec((CONV_W, tf), lambda i, j: (0, j)),
                  pl.BlockSpec((1, tf), lambda i, j: (0, j))],
        out_specs=row(d),
        out_shape=jax.ShapeDtypeStruct((n, d), F32),
        scratch_shapes=[pltpu.VMEM((tm, d), F32)],
        compiler_params=_cparams(("parallel", "arbitrary")),
        name="conv_ffn",
    )(hf, x1, halo, w['w_ffn_gate'], w['w_ffn_upv'], w['w_ffn_down'], w['conv_w'], w['conv_b'])


def _pad_cols(a, width):
    return jnp.pad(a, ((0, 0), (0, width - a.shape[1])))


def _prep_weights(attn_norm, w_in, q_a_norm, w_q_up, kv_a_norm, w_kv_up, mla_q_nope_norm,
                  mla_q_rope_norm, mla_k_nope_norm, mla_k_rope_norm, dsa_q_norm, dsa_k_norm,
                  w_o_mla, w_o_dsa, w_out, ffn_norm, w_ffn_up, conv_w, conv_b, w_ffn_down):
    d = w_in.shape[0]
    widths = (Q_LORA, KV_LORA + MLA_ROPE, DSA_HEADS * DSA_HD, 2 * DSA_KV_HEADS * DSA_HD,
              IDX_HEADS * IDX_DIM, IDX_DIM, IDX_HEADS, 2 * d)
    offs = [0]
    for wd in widths:
        offs.append(offs[-1] + wd)
    w_in = w_in.astype(BF16)
    seg = [w_in[:, offs[i]:offs[i + 1]] for i in range(len(widths))]
    w = {}
    w['w_cq'] = seg[0]
    w['w_sm'] = jnp.concatenate(
        [seg[1][:, :KV_LORA], _pad_cols(seg[1][:, KV_LORA:], LANES), _pad_cols(seg[5], LANES)],
        axis=1)
    w['w_wit'] = seg[6].T
    w['w_qd'] = seg[2]
    w['w_kvd'] = seg[3]
    w['w_qi'] = jnp.pad(seg[4].reshape(d, IDX_HEADS, IDX_DIM),
                        ((0, 0), (0, 0), (0, LANES - IDX_DIM))).reshape(d, IDX_HEADS * LANES)
    w['w_gm'] = seg[7][:, :d]
    w['w_gd'] = seg[7][:, d:]
    qup = w_q_up.astype(BF16).reshape(Q_LORA, MLA_HEADS, MLA_NOPE + MLA_ROPE)
    w['w_qup'] = jnp.pad(qup, ((0, 0), (0, 0), (0, MLA_QK_PAD - MLA_NOPE - MLA_ROPE))
                         ).reshape(Q_LORA, MLA_HEADS * MLA_QK_PAD)
    kvup = w_kv_up.astype(BF16).reshape(KV_LORA, MLA_HEADS, MLA_NOPE + MLA_V)
    w['w_kvup'] = jnp.concatenate(
        [kvup[:, :, :MLA_NOPE].reshape(KV_LORA, MLA_HEADS * MLA_NOPE),
         kvup[:, :, MLA_NOPE:].reshape(KV_LORA, MLA_HEADS * MLA_V)], axis=1)
    r2 = lambda a: a.reshape(1, -1).astype(F32)
    w['attn_norm'] = r2(attn_norm)
    w['q_a_norm'] = r2(q_a_norm)
    w['kv_a_norm'] = r2(kv_a_norm)
    w['g_kpe'] = _pad_cols(r2(mla_k_rope_norm), LANES)
    w['g_qnope'] = r2(mla_q_nope_norm)
    w['g_qrope'] = _pad_cols(r2(mla_q_rope_norm), LANES)
    w['g_knope'] = r2(mla_k_nope_norm)
    w['dsa_q_norm'] = r2(dsa_q_norm)
    w['dsa_k_norm'] = r2(dsa_k_norm)
    w['ffn_norm'] = r2(ffn_norm)
    w['w_om'] = w_o_mla.astype(BF16)
    w['w_od'] = w_o_dsa.astype(BF16)
    w['w_out'] = w_out.astype(BF16)
    w['w_ffn_gate'] = w_ffn_up[:, :D_FF].astype(BF16)
    w['w_ffn_upv'] = w_ffn_up[:, D_FF:].astype(BF16)
    w['w_ffn_down'] = w_ffn_down.astype(BF16)
    w['conv_w'] = conv_w.astype(F32)
    w['conv_b'] = r2(conv_b)
    return w


def _rope_tables(pos, rot):
    half = rot // 2
    inv = ROPE_THETA ** (-jnp.arange(half, dtype=F32) / half)
    ang = pos.astype(F32)[:, None] * inv[None, :]
    cos, sin = jnp.cos(ang), jnp.sin(ang)
    t = pos.shape[0]
    c = jnp.concatenate([cos, cos, jnp.ones((t, LANES - rot), F32)], axis=1)
    s = jnp.concatenate([-sin, sin, jnp.zeros((t, LANES - rot), F32)], axis=1)
    return c, s


def _round_up(a, m):
    return (a + m - 1) // m * m


def _tile(n, target):
    t = min(n, target)
    assert n % t == 0, (n, t)
    return t


def _layer(x, pos0, past, w):
    b, t_len, d = x.shape
    n = b * t_len
    xf = x.reshape(n, d)

    tm = _tile(n, 512)
    tm_in = _tile(n, 256)
    assert tm % t_len == 0 or t_len % tm == 0
    assert tm_in % t_len == 0 or t_len % tm_in == 0
    pos = pos0 + jnp.arange(t_len, dtype=jnp.int32)
    tabs = []
    for rot in (MLA_ROPE, DSA_ROT, IDX_ROT):
        c, s = _rope_tables(pos, rot)
        if t_len < tm_in:
            c, s = jnp.tile(c, (tm_in // t_len, 1)), jnp.tile(s, (tm_in // t_len, 1))
        tabs += [c, s]

    (h, qfull, ckv, kpe, kpeb, qd, knew, vnew, knewb, vnewb, qi, ki, kib, wit) = _in_proj(
        xf, tabs, w, tm_in, t_len)

    if past is None:
        s_len = t_len
        tk = _tile(s_len, 512)
        s_pad = s_len
        all_ckv, all_kpeb, all_kb, all_vb, all_kib = ckv, kpeb, knewb, vnewb, kib
        conv_hist = jnp.zeros((b, CONV_W - 1, D_FF), F32)
    else:
        p_ckv, p_kpe, p_k, p_v, p_ki, conv_hist = past
        past_len = p_ckv.shape[1]
        s_len = past_len + t_len
        tk = 384 if s_len > 384 else _round_up(s_len, LANES)
        s_pad = _round_up(s_len, tk)
        padr = s_pad - s_len

        def cat(p_arr, new, width, dtype, lane_pad=0):
            p2 = p_arr.reshape(b, past_len, width).astype(dtype)
            if lane_pad:
                p2 = jnp.pad(p2, ((0, 0), (0, 0), (0, lane_pad)))
            a = jnp.concatenate([p2, new.reshape(b, t_len, width + lane_pad)], axis=1)
            return jnp.pad(a, ((0, 0), (0, padr), (0, 0))).reshape(b * s_pad, width + lane_pad)

        all_ckv = cat(p_ckv, ckv, KV_LORA, F32)
        all_kpeb = cat(p_kpe, kpeb, MLA_ROPE, BF16, LANES - MLA_ROPE)
        all_kb = cat(p_k, knewb, DSA_KV_HEADS * DSA_HD, BF16)
        all_vb = cat(p_v, vnewb, DSA_KV_HEADS * DSA_HD, BF16)
        all_kib = cat(p_ki, kib, IDX_DIM, BF16, LANES - IDX_DIM)

    kfull, vt_mla = _kv_up(all_ckv, all_kpeb, w, tk)
    tq = _tile(t_len, 256)
    n_ch = s_pad // tk
    vt_dsa = all_vb.reshape(b * n_ch, tk, DSA_KV_HEADS, DSA_HD).transpose(0, 2, 3, 1)
    o_mla = _mla_attn(qfull, kfull, vt_mla, b, t_len, s_pad, tq, tk, pos0, s_len, 4)
    topk = min(TOPK_MAX, s_len // 4)
    o_dsa = _dsa_attn(qi, wit, qd, all_kib, all_kb, vt_dsa, b, t_len, s_pad, tq, tk, pos0, s_len,
                      topk)

    x1, hf = _merge_out(h, o_mla, o_dsa, xf, w, tm, 512)

    t_seg = min(t_len, tm)
    assert t_seg & (t_seg - 1) == 0 and t_seg >= CONV_W - 1
    n_t = t_len // t_seg
    segs = tm // t_seg
    brows = hf.reshape(b, n_t, t_seg, d)[:, :, t_seg - (CONV_W - 1):, :].reshape(-1, d)
    bg = _gate_rows(brows, w['w_ffn_gate'], 512).reshape(b, n_t, CONV_W - 1, D_FF)
    halo = jnp.concatenate([conv_hist.astype(F32)[:, None], bg[:, :-1]], axis=1)
    halo = halo.reshape(n // tm, segs * (CONV_W - 1), D_FF)
    halo_rows = _round_up(halo.shape[1], 16)
    halo = jnp.pad(halo, ((0, 0), (0, halo_rows - halo.shape[1]), (0, 0)))
    y = _ffn(hf, x1, halo, w, tm, 512, t_seg)

    state = (ckv.reshape(b, t_len, KV_LORA), kpe.reshape(b, t_len, MLA_ROPE),
             knew.reshape(b, t_len, DSA_KV_HEADS, DSA_HD), vnew.reshape(b, t_len, DSA_KV_HEADS, DSA_HD),
             ki.reshape(b, t_len, IDX_DIM), bg[:, -1])
    return y.reshape(b, t_len, d), state


def kernel(x_prompt, x_sample, cache_mla_ckv, cache_mla_kpe, cache_dsa_k, cache_dsa_v, cache_idx_k, state_ffn_conv, attn_norm, w_in, q_a_norm, w_q_up, kv_a_norm, w_kv_up, mla_q_nope_norm, mla_q_rope_norm, mla_k_nope_norm, mla_k_rope_norm, dsa_q_norm, dsa_k_norm, w_o_mla, w_o_dsa, w_out, ffn_norm, w_ffn_up, conv_w, conv_b, w_ffn_down):
    depth = w_in.shape[0]
    past_len = cache_mla_ckv.shape[2]
    xp, xs = x_prompt, x_sample
    p_states, s_states = [], []
    for l in range(depth):
        w = _prep_weights(attn_norm[l], w_in[l], q_a_norm[l], w_q_up[l], kv_a_norm[l], w_kv_up[l],
                          mla_q_nope_norm[l], mla_q_rope_norm[l], mla_k_nope_norm[l],
                          mla_k_rope_norm[l], dsa_q_norm[l], dsa_k_norm[l], w_o_mla[l], w_o_dsa[l],
                          w_out[l], ffn_norm[l], w_ffn_up[l], conv_w[l], conv_b[l], w_ffn_down[l])
        past = (cache_mla_ckv[l], cache_mla_kpe[l], cache_dsa_k[l], cache_dsa_v[l],
                cache_idx_k[l], state_ffn_conv[l])
        xp, sp = _layer(xp, 0, None, w)
        xs, ss = _layer(xs, past_len, past, w)
        p_states.append(sp)
        s_states.append(ss)
    p_out = [jnp.stack(a) for a in zip(*p_states)]
    s_out = [jnp.stack(a) for a in zip(*s_states)]
    return (xp, xs, *p_out, *s_out)
```

```python
import functools

import jax
import jax.numpy as jnp
from jax import lax
from jax.experimental import pallas as pl
from jax.experimental.pallas import tpu as pltpu

CHUNK = 64
CHUNK_SHIFT = 6
ROPE_THETA = 500000.0
NORM_EPS = 1e-6
NEG_INF = -1e30
LOG2E = 1.4426950408889634
MLA_HEADS = 8
Q_LORA = 512
KV_LORA = 256
MLA_NOPE = 128
MLA_ROPE = 64
MLA_V = 128
MLA_QK_PAD = 256
DSA_HEADS = 8
DSA_KV_HEADS = 2
DSA_HD = 128
DSA_ROT = DSA_HD // 4
IDX_HEADS = 16
IDX_DIM = 64
IDX_ROT = IDX_DIM // 4
TOPK_MAX = 256
D_FF = 5632
CONV_W = 3
MLA_LOOKAHEAD = 6
DSA_LOOKAHEAD = 4
CHUNK_UNROLL = 2
COUNT_ROWS = 32

LANES = 128
SUBLANES = 8
VMEM_LIMIT = 56 * 1024 * 1024
INT_MIN = -2 ** 31

F32 = jnp.float32
BF16 = jnp.bfloat16


def _cparams(sem, flags=None):
    return pltpu.CompilerParams(dimension_semantics=sem, vmem_limit_bytes=VMEM_LIMIT, flags=flags)


def _resident(shape):
    nd = len(shape)
    return pl.BlockSpec(shape, lambda *_: (0,) * nd, pipeline_mode=pl.Buffered(1))


def _rms(x, g, n=None):
    n = x.shape[-1] if n is None else n
    ms = jnp.sum(x * x, axis=-1, keepdims=True) * (1.0 / n)
    return x * lax.rsqrt(ms + NORM_EPS) * g


def _rope128(x, c, s, half):
    lane = lax.broadcasted_iota(jnp.int32, x.shape, 1)
    swapped = jnp.where(lane < half, pltpu.roll(x, LANES - half, 1), pltpu.roll(x, half, 1))
    return x * c + swapped * s


def _dot(a, b):
    return jnp.dot(a, b, preferred_element_type=F32)


def _dot_t(a, b):
    return lax.dot_general(a, b, (((1,), (1,)), ((), ())), preferred_element_type=F32)


def _in_proj_kernel(x_ref, gattn_ref, wcq_ref, wsm_ref, wqd_ref, wkvd_ref, wqi_ref, wqup_ref,
                    gqa_ref, gkva_ref, gkpe_ref, gqn_ref, gqr_ref, gdq_ref, gdk_ref,
                    c64_ref, s64_ref, c32_ref, s32_ref, c16_ref, s16_ref,
                    h_ref, qfull_ref, ckv_ref, kpe_ref, kpeb_ref, qd_ref, knew_ref, vnew_ref,
                    knewb_ref, vnewb_ref, qi_ref, ki_ref, kib_ref, wit_ref):
    x = x_ref[...]
    hb = _rms(x, gattn_ref[...]).astype(BF16)
    h_ref[...] = hb
    c64, s64 = c64_ref[...], s64_ref[...]
    c32, s32 = c32_ref[...], s32_ref[...]
    c16, s16 = c16_ref[...], s16_ref[...]

    cqn = _rms(_dot(hb, wcq_ref[...]), gqa_ref[...]).astype(BF16)
    q = _dot(cqn, wqup_ref[...])
    scale = (MLA_NOPE + MLA_ROPE) ** -0.5 * LOG2E
    for hd in range(MLA_HEADS):
        lo = hd * MLA_QK_PAD
        nope = _rms(q[:, lo:lo + LANES], gqn_ref[...]) * scale
        qfull_ref[:, lo:lo + LANES] = nope.astype(BF16)
        pe = _rms(q[:, lo + LANES:lo + 2 * LANES], gqr_ref[...], MLA_ROPE)
        pe = _rope128(pe, c64, s64, MLA_ROPE // 2) * scale
        qfull_ref[:, lo + LANES:lo + 2 * LANES] = pe.astype(BF16)

    sm = _dot(hb, wsm_ref[...])
    ckv_ref[...] = _rms(sm[:, :KV_LORA], gkva_ref[...])
    kpe = _rope128(_rms(sm[:, KV_LORA:KV_LORA + LANES], gkpe_ref[...], MLA_ROPE), c64, s64,
                   MLA_ROPE // 2)
    kpe_ref[...] = kpe[:, :MLA_ROPE]
    kpeb_ref[...] = kpe.astype(BF16)
    ki = _rope128(sm[:, KV_LORA + LANES:KV_LORA + 2 * LANES], c16, s16, IDX_ROT // 2)
    ki_ref[...] = ki[:, :IDX_DIM]
    kib_ref[...] = ki.astype(BF16)
    wi = sm[:, KV_LORA + 2 * LANES:KV_LORA + 3 * LANES] * (IDX_HEADS ** -0.5)
    wit_ref[...] = wi.T[:IDX_HEADS]

    qd = _dot(hb, wqd_ref[...])
    for hd in range(DSA_HEADS):
        lo = hd * DSA_HD
        t = _rope128(_rms(qd[:, lo:lo + DSA_HD], gdq_ref[...]), c32, s32, DSA_ROT // 2)
        qd_ref[:, lo:lo + DSA_HD] = (t * (DSA_HD ** -0.5 * LOG2E)).astype(BF16)
    kvd = _dot(hb, wkvd_ref[...])
    for hd in range(DSA_KV_HEADS):
        lo = hd * DSA_HD
        t = _rope128(_rms(kvd[:, lo:lo + DSA_HD], gdk_ref[...]), c32, s32, DSA_ROT // 2)
        knew_ref[:, lo:lo + DSA_HD] = t
        knewb_ref[:, lo:lo + DSA_HD] = t.astype(BF16)
    v = kvd[:, DSA_KV_HEADS * DSA_HD:]
    vnew_ref[...] = v
    vnewb_ref[...] = v.astype(BF16)

    qi = _dot(hb, wqi_ref[...])
    for hd in range(IDX_HEADS):
        lo = hd * LANES
        t = _rope128(qi[:, lo:lo + LANES], c16, s16, IDX_ROT // 2)
        qi_ref[:, lo:lo + LANES] = (t * (IDX_DIM ** -0.5)).astype(BF16)


def _in_proj(xf, tabs, w, tm, t_len):
    n, d = xf.shape
    n_tiles = n // tm
    if t_len >= tm:
        per = t_len // tm
        tab_map = lambda i: (i % per, 0)
    else:
        tab_map = lambda i: (0, 0)
    row = lambda width: pl.BlockSpec((tm, width), lambda i: (i, 0))
    tab = pl.BlockSpec((tm, LANES), tab_map)
    weights = [w['attn_norm'], w['w_cq'], w['w_sm'], w['w_qd'], w['w_kvd'], w['w_qi'], w['w_qup'],
               w['q_a_norm'], w['kv_a_norm'], w['g_kpe'], w['g_qnope'], w['g_qrope'],
               w['dsa_q_norm'], w['dsa_k_norm']]
    out_widths = [(d, BF16), (MLA_HEADS * MLA_QK_PAD, BF16), (KV_LORA, F32), (MLA_ROPE, F32),
                  (LANES, BF16), (DSA_HEADS * DSA_HD, BF16), (DSA_KV_HEADS * DSA_HD, F32),
                  (DSA_KV_HEADS * DSA_HD, F32), (DSA_KV_HEADS * DSA_HD, BF16),
                  (DSA_KV_HEADS * DSA_HD, BF16), (IDX_HEADS * LANES, BF16), (IDX_DIM, F32),
                  (LANES, BF16)]
    return pl.pallas_call(
        _in_proj_kernel,
        grid=(n_tiles,),
        in_specs=[row(d)] + [_resident(a.shape) for a in weights] + [tab] * 6,
        out_specs=[row(wd) for wd, _ in out_widths]
        + [pl.BlockSpec((IDX_HEADS, tm), lambda i: (0, i))],
        out_shape=[jax.ShapeDtypeStruct((n, wd), dt) for wd, dt in out_widths]
        + [jax.ShapeDtypeStruct((IDX_HEADS, n), F32)],
        compiler_params=_cparams(("parallel",)),
        name="in_proj",
    )(xf, *weights, *tabs)


def _kv_up_kernel(ckv_ref, kpe_ref, w_ref, g_ref, kfull_ref, vt_ref):
    kv = _dot(ckv_ref[...].astype(BF16), w_ref[...])
    kpe = kpe_ref[...]
    g = g_ref[...]
    for hd in range(MLA_HEADS):
        lo = hd * MLA_QK_PAD
        kfull_ref[:, lo:lo + LANES] = _rms(kv[:, hd * LANES:(hd + 1) * LANES], g).astype(BF16)
        kfull_ref[:, lo + LANES:lo + 2 * LANES] = kpe
    for hd in range(MLA_HEADS):
        lo = MLA_HEADS * MLA_NOPE + hd * MLA_V
        vt_ref[0, hd] = kv[:, lo:lo + MLA_V].T.astype(BF16)


def _kv_up(ckv, kpeb, w, tm):
    n = ckv.shape[0]
    row = lambda width: pl.BlockSpec((tm, width), lambda i: (i, 0))
    return pl.pallas_call(
        _kv_up_kernel,
        grid=(n // tm,),
        in_specs=[row(KV_LORA), row(LANES), _resident(w['w_kvup'].shape),
                  _resident(w['g_knope'].shape)],
        out_specs=[row(MLA_HEADS * MLA_QK_PAD),
                   pl.BlockSpec((1, MLA_HEADS, MLA_V, tm), lambda i: (i, 0, 0, 0))],
        out_shape=[jax.ShapeDtypeStruct((n, MLA_HEADS * MLA_QK_PAD), BF16),
                   jax.ShapeDtypeStruct((n // tm, MLA_HEADS, MLA_V, tm), BF16)],
        compiler_params=_cparams(("parallel",)),
        name="kv_up",
    )(ckv, kpeb, w['w_kvup'], w['g_knope'])


def _visible_limit(q_last_pos, s_valid):
    return jnp.minimum(((q_last_pos >> CHUNK_SHIFT) + 1) * CHUNK, s_valid)


def _softmax_step(s, vt, m, l, acc):
    m_new = jnp.maximum(m, jnp.max(s, axis=0, keepdims=True))
    alpha = jnp.exp2(m - m_new)
    p = jnp.exp2(s - m_new)
    l = alpha * l + jnp.sum(p, axis=0, keepdims=True)
    acc = alpha * acc + _dot(vt, p.astype(BF16))
    return m_new, l, acc


def _run_tiles(tiles, score_fn, value_fn, carry, lookahead):
    carry = list(carry)
    pending = [score_fn(*t) for t in tiles[:lookahead]]
    for i, (stream, chunk) in enumerate(tiles):
        s = pending.pop(0)
        carry[stream] = _softmax_step(s, value_fn(stream, chunk), *carry[stream])
        if i + lookahead < len(tiles):
            pending.append(score_fn(*tiles[i + lookahead]))
    return tuple(carry)


def _softmax_init(tq, dv, streams):
    return tuple((jnp.full((1, tq), NEG_INF, F32), jnp.zeros((1, tq), F32),
                  jnp.zeros((dv, tq), F32)) for _ in range(streams))


def _lane_width(tq):
    return max(tq, LANES)


def _transposed(block):
    x = block.astype(F32)
    tq, d = x.shape
    if tq < _lane_width(tq):
        x = jnp.concatenate([x, jnp.zeros((_lane_width(tq) - tq, d), F32)], axis=0)
    return x.T.astype(BF16)


def _visibility(start, tk, qchunk, s_valid):
    kpos = start + lax.broadcasted_iota(jnp.int32, (tk, 1), 0)
    return ((kpos >> CHUNK_SHIFT) <= qchunk) & (kpos < s_valid)


def _mla_kernel(q_ref, k_ref, vt_ref, o_ref, *, tq, tk, pos0, s_valid, nh):
    qt = pl.program_id(2)
    q_first = pos0 + qt * tq
    tql = _lane_width(tq)
    qchunk = (q_first + lax.broadcasted_iota(jnp.int32, (1, tql), 1)) >> CHUNK_SHIFT
    n_chunks = pl.cdiv(_visible_limit(q_first + tq - 1, s_valid), tk)
    n_full = _visible_limit(q_first, s_valid) // tk
    qts = [_transposed(q_ref[:, hd * MLA_QK_PAD:(hd + 1) * MLA_QK_PAD]) for hd in range(nh)]

    def make_body(masked, unroll):
        def body(cu, carry):
            def score(hd, u):
                start = pl.multiple_of((cu * unroll + u) * tk, tk)
                s = _dot(k_ref[pl.ds(start, tk), hd * MLA_QK_PAD:(hd + 1) * MLA_QK_PAD], qts[hd])
                if masked:
                    s = jnp.where(_visibility(start, tk, qchunk, s_valid), s, NEG_INF)
                return s

            def value(hd, u):
                return vt_ref[cu * unroll + u, hd]

            tiles = [(hd, u) for u in range(unroll) for hd in range(nh)]
            return _run_tiles(tiles, score, value, carry, MLA_LOOKAHEAD)
        return body

    n_pairs = n_full // CHUNK_UNROLL
    carry = lax.fori_loop(0, n_pairs, make_body(False, CHUNK_UNROLL),
                          _softmax_init(tql, MLA_V, nh))
    carry = lax.fori_loop(n_pairs * CHUNK_UNROLL, n_chunks, make_body(True, 1), carry)
    for hd in range(nh):
        _, l, acc = carry[hd]
        o_ref[:, hd * MLA_V:(hd + 1) * MLA_V] = (acc / l).T[:tq].astype(BF16)


def _mla_attn(qfull, kfull, vt, b, t_len, s_pad, tq, tk, pos0, s_valid, nh):
    nq = t_len // tq
    n_ch = s_pad // tk
    kern = functools.partial(_mla_kernel, tq=tq, tk=tk, pos0=pos0, s_valid=s_valid, nh=nh)
    return pl.pallas_call(
        kern,
        grid=(b, MLA_HEADS // nh, nq),
        in_specs=[pl.BlockSpec((tq, nh * MLA_QK_PAD), lambda bi, h, i: (bi * nq + i, h)),
                  pl.BlockSpec((s_pad, nh * MLA_QK_PAD), lambda bi, h, i: (bi, h)),
                  pl.BlockSpec((n_ch, nh, MLA_V, tk), lambda bi, h, i: (bi, h, 0, 0))],
        out_specs=pl.BlockSpec((tq, nh * MLA_V), lambda bi, h, i: (bi * nq + i, h)),
        out_shape=jax.ShapeDtypeStruct((b * t_len, MLA_HEADS * MLA_V), BF16),
        compiler_params=_cparams(("parallel", "parallel", "arbitrary")),
        name="mla_attn",
    )(qfull, kfull, vt)


def _count_keys(preds, key_scr, n_chunks, tq, tk):
    def make_body(unroll):
        def body(cu, parts):
            parts = list(parts)
            for u in range(unroll):
                c = cu * unroll + u
                keys = key_scr[c]
                idx = c * tk + lax.broadcasted_iota(jnp.int32, (tk, 1), 0)
                for n, pred in enumerate(preds):
                    hit = jnp.where(pred(keys, idx), 1, 0).reshape(tk // COUNT_ROWS, COUNT_ROWS, tq)
                    parts[n] = parts[n] + jnp.sum(hit, axis=0)
            return tuple(parts)
        return body
    zero = jnp.zeros((COUNT_ROWS, tq), jnp.int32)
    n_pairs = n_chunks // CHUNK_UNROLL
    parts = lax.fori_loop(0, n_pairs, make_body(CHUNK_UNROLL), (zero,) * len(preds))
    parts = lax.fori_loop(n_pairs * CHUNK_UNROLL, n_chunks, make_body(1), parts)
    return [jnp.sum(p, axis=0, keepdims=True) for p in parts]


def _dsa_kernel(qi_ref, wit_ref, qd_ref, ki_ref, k_ref, vt_ref, o_ref, key_scr, bias_scr, cut_scr,
                *, tq, tk, streams, pos0, s_valid, topk, idx_bits):
    qt = pl.program_id(1)
    q_first = pos0 + qt * tq
    tql = _lane_width(tq)
    lane = lax.broadcasted_iota(jnp.int32, (1, tql), 1)
    qchunk = (q_first + lane) >> CHUNK_SHIFT
    n_chunks = pl.cdiv(_visible_limit(q_first + tq - 1, s_valid), tk)
    wit = wit_ref[0]
    if tq < tql:
        wit = jnp.concatenate([wit, jnp.zeros((IDX_HEADS, tql - tq), F32)], axis=1)
    qits = [_transposed(qi_ref[:, hd * LANES:(hd + 1) * LANES]) for hd in range(IDX_HEADS)]

    def score_body(c, _):
        start = pl.multiple_of(c * tk, tk)
        kic = ki_ref[pl.ds(start, tk), :]
        sc = jnp.zeros((tk, tql), F32)
        for hd in range(IDX_HEADS):
            sc = sc + wit[hd:hd + 1, :] * jnp.maximum(_dot(kic, qits[hd]), 0.0)
        bits = pltpu.bitcast(sc + 0.0, jnp.int32)
        keys = bits ^ ((bits >> 31) & 0x7FFFFFFF)
        key_scr[c] = jnp.where(_visibility(start, tk, qchunk, s_valid), keys, INT_MIN)
        return 0
    lax.fori_loop(0, n_chunks, score_body, 0)

    def bit_body(b, t):
        cand = t + (jnp.int32(1) << (31 - b))
        cnt, = _count_keys([lambda kk, _: kk >= cand], key_scr, n_chunks, tql, tk)
        return jnp.where(cnt >= topk, cand, t)
    thr = lax.fori_loop(0, 32, bit_body, jnp.full((1, tql), INT_MIN, jnp.int32))

    cnt_gt, cnt_eq = _count_keys([lambda kk, _: kk > thr, lambda kk, _: kk == thr],
                                 key_scr, n_chunks, tql, tk)
    need = topk - cnt_gt
    cut_scr[...] = jnp.full((SUBLANES, tql), 2 ** 30, jnp.int32)
    excess = jnp.max(jnp.where((cnt_eq > need) & (thr > INT_MIN) & (lane < tq), 1, 0))

    @pl.when(excess > 0)
    def _():
        def idx_body(b, pos):
            cand = pos + (jnp.int32(1) << (idx_bits - 1 - b))
            cnt, = _count_keys([lambda kk, idx: (kk == thr) & (idx < cand)], key_scr, n_chunks,
                               tql, tk)
            return jnp.where(cnt < need, cand, pos)
        pos = lax.fori_loop(0, idx_bits, idx_body, jnp.zeros((1, tql), jnp.int32))
        cut_scr[...] = jnp.broadcast_to(pos, (SUBLANES, tql))
    cut = jnp.where(thr > INT_MIN, cut_scr[0:1, :], -1)

    def bias_body(c, _):
        keys = key_scr[c]
        idx = c * tk + lax.broadcasted_iota(jnp.int32, (tk, 1), 0)
        tie = jnp.where(idx <= cut, 0.0, NEG_INF)
        bias_scr[c] = jnp.where(keys > thr, 0.0, jnp.where(keys == thr, tie, NEG_INF))
        return 0
    lax.fori_loop(0, n_chunks, bias_body, 0)

    rep = DSA_HEADS // DSA_KV_HEADS
    for h0 in range(0, DSA_HEADS, streams):
        qts = [_transposed(qd_ref[:, hd * DSA_HD:(hd + 1) * DSA_HD]) for hd in range(h0, h0 + streams)]

        def make_att_body(unroll, qts=qts, h0=h0):
            def att_body(cu, carry):
                def score(n, u):
                    c = cu * unroll + u
                    start = pl.multiple_of(c * tk, tk)
                    g = (h0 + n) // rep
                    kc = k_ref[pl.ds(start, tk), g * DSA_HD:(g + 1) * DSA_HD]
                    return _dot(kc, qts[n]) + bias_scr[c]

                def value(n, u):
                    return vt_ref[cu * unroll + u, (h0 + n) // rep]

                tiles = [(n, u) for u in range(unroll) for n in range(streams)]
                return _run_tiles(tiles, score, value, carry, DSA_LOOKAHEAD)
            return att_body

        n_pairs = n_chunks // CHUNK_UNROLL
        carry = lax.fori_loop(0, n_pairs, make_att_body(CHUNK_UNROLL),
                              _softmax_init(tql, DSA_HD, streams))
        carry = lax.fori_loop(n_pairs * CHUNK_UNROLL, n_chunks, make_att_body(1), carry)
        for n, (_, l, acc) in enumerate(carry):
            hd = h0 + n
            o_ref[:, hd * DSA_HD:(hd + 1) * DSA_HD] = (acc / l).T[:tq].astype(BF16)


def _dsa_attn(qi, wit, qd, kib, kb, vt, b, t_len, s_pad, tq, tk, pos0, s_valid, topk):
    nq = t_len // tq
    n_ch = s_pad // tk
    idx_bits = max(1, (s_pad - 1).bit_length())
    tql = _lane_width(tq)
    wit = wit.reshape(IDX_HEADS, b * nq, tq).transpose(1, 0, 2)
    kern = functools.partial(_dsa_kernel, tq=tq, tk=tk, streams=8, pos0=pos0, s_valid=s_valid,
                             topk=topk, idx_bits=idx_bits)
    qrow = lambda width: pl.BlockSpec((tq, width), lambda bi, i: (bi * nq + i, 0))
    krow = lambda width: pl.BlockSpec((s_pad, width), lambda bi, i: (bi, 0))
    return pl.pallas_call(
        kern,
        grid=(b, nq),
        in_specs=[qrow(IDX_HEADS * LANES),
                  pl.BlockSpec((1, IDX_HEADS, tq), lambda bi, i: (bi * nq + i, 0, 0)),
                  qrow(DSA_HEADS * DSA_HD), krow(LANES), krow(DSA_KV_HEADS * DSA_HD),
                  pl.BlockSpec((n_ch, DSA_KV_HEADS, DSA_HD, tk), lambda bi, i: (bi, 0, 0, 0))],
        out_specs=qrow(DSA_HEADS * DSA_HD),
        out_shape=jax.ShapeDtypeStruct((b * t_len, DSA_HEADS * DSA_HD), BF16),
        scratch_shapes=[pltpu.VMEM((n_ch, tk, tql), jnp.int32), pltpu.VMEM((n_ch, tk, tql), F32),
                        pltpu.VMEM((SUBLANES, tql), jnp.int32)],
        compiler_params=_cparams(("parallel", "arbitrary")),
        name="dsa_attn",
    )(qi, wit, qd, kib, kb, vt)


def _sigmoid(x):
    return 1.0 / (1.0 + jnp.exp(-x))


def _merge_out_kernel(h_ref, om_ref, od_ref, x_ref, wgm_ref, wgd_ref, wom_ref, wod_ref, wout_ref,
                      gffn_ref, x1_ref, hf_ref, acc_scr):
    j = pl.program_id(1)

    @pl.when(j == 0)
    def _():
        acc_scr[...] = x_ref[...]

    h = h_ref[...]
    merged = (_sigmoid(_dot(h, wgm_ref[...])) * _dot(om_ref[...], wom_ref[...])
              + _sigmoid(_dot(h, wgd_ref[...])) * _dot(od_ref[...], wod_ref[...]))
    acc_scr[...] += _dot(merged.astype(BF16), wout_ref[...])

    @pl.when(j == pl.num_programs(1) - 1)
    def _():
        x1 = acc_scr[...]
        x1_ref[...] = x1
        hf_ref[...] = _rms(x1, gffn_ref[...]).astype(BF16)


def _merge_out(h, om, od, xf, w, tm, tn):
    n, d = xf.shape
    row = lambda width: pl.BlockSpec((tm, width), lambda i, j: (i, 0))
    col = lambda rows: pl.BlockSpec((rows, tn), lambda i, j: (0, j))
    return pl.pallas_call(
        _merge_out_kernel,
        grid=(n // tm, d // tn),
        in_specs=[row(d), row(om.shape[1]), row(od.shape[1]), row(d),
                  col(d), col(d), col(om.shape[1]), col(od.shape[1]),
                  pl.BlockSpec((tn, d), lambda i, j: (j, 0)),
                  pl.BlockSpec((1, d), lambda i, j: (0, 0))],
        out_specs=[row(d), row(d)],
        out_shape=[jax.ShapeDtypeStruct((n, d), F32), jax.ShapeDtypeStruct((n, d), BF16)],
        scratch_shapes=[pltpu.VMEM((tm, d), F32)],
        compiler_params=_cparams(("parallel", "arbitrary")),
        name="merge_out",
    )(h, om, od, xf, w['w_gm'], w['w_gd'], w['w_om'], w['w_od'], w['w_out'], w['ffn_norm'])


def _gate_rows_kernel(h_ref, w_ref, o_ref):
    o_ref[...] = _dot(h_ref[...], w_ref[...])


def _gate_rows(rows, w_gate, tf):
    r, d = rows.shape
    return pl.pallas_call(
        _gate_rows_kernel,
        grid=(D_FF // tf,),
        in_specs=[pl.BlockSpec((r, d), lambda j: (0, 0)), pl.BlockSpec((d, tf), lambda j: (0, j))],
        out_specs=pl.BlockSpec((r, tf), lambda j: (0, j)),
        out_shape=jax.ShapeDtypeStruct((r, D_FF), F32),
        compiler_params=_cparams(("parallel",)),
        name="ffn_gate_rows",
    )(rows, w_gate)


def _split3(x):
    hi = x.astype(BF16)
    r1 = x - hi.astype(F32)
    mid = r1.astype(BF16)
    lo = (r1 - mid.astype(F32)).astype(BF16)
    return hi, mid, lo


def _ffn_kernel(hf_ref, x1_ref, halo_ref, wg_ref, wu_ref, wd_ref, cw_ref, cb_ref, y_ref, acc_scr,
                *, tm, t_seg, halo_rows):
    j = pl.program_id(1)

    @pl.when(j == 0)
    def _():
        acc_scr[...] = x1_ref[...]

    hf = hf_ref[...]

    r1 = lax.broadcasted_iota(jnp.int32, (tm, 1), 0) & (t_seg - 1)
    halo = halo_ref[0]
    if t_seg == tm:
        fix1 = halo[1:2, :]
        fix2 = jnp.where(r1 < 1, halo[0:1, :], halo[1:2, :])
    else:
        seg_shift = t_seg.bit_length() - 1
        row = lax.broadcasted_iota(jnp.int32, (tm, halo_rows), 0)
        colh = lax.broadcasted_iota(jnp.int32, (tm, halo_rows), 1)
        seg2 = (row >> seg_shift) * 2
        r_in = row & (t_seg - 1)
        e1 = jnp.where((r_in == 0) & (colh == seg2 + 1), 1.0, 0.0).astype(BF16)
        e2 = jnp.where(((r_in == 0) & (colh == seg2)) | ((r_in == 1) & (colh == seg2 + 1)),
                       1.0, 0.0).astype(BF16)
        pieces = _split3(halo)
        fix1 = _dot(e1, pieces[0]) + _dot(e1, pieces[1]) + _dot(e1, pieces[2])
        fix2 = _dot(e2, pieces[0]) + _dot(e2, pieces[1]) + _dot(e2, pieces[2])
    cw = cw_ref[...]
    cb = cb_ref[...]

    g = _dot(hf, wg_ref[...])
    up = _dot(hf, wu_ref[...])
    g1 = jnp.where(r1 < 1, fix1, pltpu.roll(g, 1, 0))
    g2 = jnp.where(r1 < 2, fix2, pltpu.roll(g, 2, 0))
    conv = cb + cw[0:1, :] * g2 + cw[1:2, :] * g1 + cw[2:3, :] * g
    act = conv * _sigmoid(conv) * up
    acc_scr[...] += _dot(act.astype(BF16), wd_ref[...])

    @pl.when(j == pl.num_programs(1) - 1)
    def _():
        y_ref[...] = acc_scr[...]


def _ffn(hf, x1, halo, w, tm, tf, t_seg):
    n, d = x1.shape
    halo_rows = halo.shape[1]
    kern = functools.partial(_ffn_kernel, tm=tm, t_seg=t_seg, halo_rows=halo_rows)
    row = lambda width: pl.BlockSpec((tm, width), lambda i, j: (i, 0))
    return pl.pallas_call(
        kern,
        grid=(n // tm, D_FF // tf),
        in_specs=[row(d), row(d),
                  pl.BlockSpec((1, halo_rows, tf), lambda i, j: (i, 0, j)),
                  pl.BlockSpec((d, tf), lambda i, j: (0, j)),
                  pl.BlockSpec((d, tf), lambda i, j: (0, j)),
                  pl.BlockSpec((tf, d), lambda i, j: (j, 0)),
                  pl.BlockSpec((CONV_W, tf), lambda i, j: (0, j)),
                  pl.BlockSpec((1, tf), lambda i, j: (0, j))],
        out_specs=row(d),
        out_shape=jax.ShapeDtypeStruct((n, d), F32),
        scratch_shapes=[pltpu.VMEM((tm, d), F32)],
        compiler_params=_cparams(("parallel", "arbitrary")),
        name="conv_ffn",
    )(hf, x1, halo, w['w_ffn_gate'], w['w_ffn_upv'], w['w_ffn_down'], w['conv_w'], w['conv_b'])


def _pad_cols(a, width):
    return jnp.pad(a, ((0, 0), (0, width - a.shape[1])))


def _prep_weights(attn_norm, w_in, q_a_norm, w_q_up, kv_a_norm, w_kv_up, mla_q_nope_norm,
                  mla_q_rope_norm, mla_k_nope_norm, mla_k_rope_norm, dsa_q_norm, dsa_k_norm,
                  w_o_mla, w_o_dsa, w_out, ffn_norm, w_ffn_up, conv_w, conv_b, w_ffn_down):
    d = w_in.shape[0]
    widths = (Q_LORA, KV_LORA + MLA_ROPE, DSA_HEADS * DSA_HD, 2 * DSA_KV_HEADS * DSA_HD,
              IDX_HEADS * IDX_DIM, IDX_DIM, IDX_HEADS, 2 * d)
    offs = [0]
    for wd in widths:
        offs.append(offs[-1] + wd)
    w_in = w_in.astype(BF16)
    seg = [w_in[:, offs[i]:offs[i + 1]] for i in range(len(widths))]
    w = {}
    w['w_cq'] = seg[0]
    w['w_sm'] = jnp.concatenate(
        [seg[1][:, :KV_LORA], _pad_cols(seg[1][:, KV_LORA:], LANES), _pad_cols(seg[5], LANES),
         _pad_cols(seg[6], LANES)], axis=1)
    w['w_qd'] = seg[2]
    w['w_kvd'] = seg[3]
    w['w_qi'] = jnp.pad(seg[4].reshape(d, IDX_HEADS, IDX_DIM),
                        ((0, 0), (0, 0), (0, LANES - IDX_DIM))).reshape(d, IDX_HEADS * LANES)
    w['w_gm'] = seg[7][:, :d]
    w['w_gd'] = seg[7][:, d:]
    qup = w_q_up.astype(BF16).reshape(Q_LORA, MLA_HEADS, MLA_NOPE + MLA_ROPE)
    w['w_qup'] = jnp.pad(qup, ((0, 0), (0, 0), (0, MLA_QK_PAD - MLA_NOPE - MLA_ROPE))
                         ).reshape(Q_LORA, MLA_HEADS * MLA_QK_PAD)
    kvup = w_kv_up.astype(BF16).reshape(KV_LORA, MLA_HEADS, MLA_NOPE + MLA_V)
    w['w_kvup'] = jnp.concatenate(
        [kvup[:, :, :MLA_NOPE].reshape(KV_LORA, MLA_HEADS * MLA_NOPE),
         kvup[:, :, MLA_NOPE:].reshape(KV_LORA, MLA_HEADS * MLA_V)], axis=1)
    r2 = lambda a: a.reshape(1, -1).astype(F32)
    w['attn_norm'] = r2(attn_norm)
    w['q_a_norm'] = r2(q_a_norm)
    w['kv_a_norm'] = r2(kv_a_norm)
    w['g_kpe'] = _pad_cols(r2(mla_k_rope_norm), LANES)
    w['g_qnope'] = r2(mla_q_nope_norm)
    w['g_qrope'] = _pad_cols(r2(mla_q_rope_norm), LANES)
    w['g_knope'] = r2(mla_k_nope_norm)
    w['dsa_q_norm'] = r2(dsa_q_norm)
    w['dsa_k_norm'] = r2(dsa_k_norm)
    w['ffn_norm'] = r2(ffn_norm)
    w['w_om'] = w_o_mla.astype(BF16)
    w['w_od'] = w_o_dsa.astype(BF16)
    w['w_out'] = w_out.astype(BF16)
    w['w_ffn_gate'] = w_ffn_up[:, :D_FF].astype(BF16)
    w['w_ffn_upv'] = w_ffn_up[:, D_FF:].astype(BF16)
    w['w_ffn_down'] = w_ffn_down.astype(BF16)
    w['conv_w'] = conv_w.astype(F32)
    w['conv_b'] = r2(conv_b)
    return w


def _rope_tables(pos, rot):
    half = rot // 2
    inv = ROPE_THETA ** (-jnp.arange(half, dtype=F32) / half)
    ang = pos.astype(F32)[:, None] * inv[None, :]
    cos, sin = jnp.cos(ang), jnp.sin(ang)
    t = pos.shape[0]
    c = jnp.concatenate([cos, cos, jnp.ones((t, LANES - rot), F32)], axis=1)
    s = jnp.concatenate([-sin, sin, jnp.zeros((t, LANES - rot), F32)], axis=1)
    return c, s


def _round_up(a, m):
    return (a + m - 1) // m * m


def _tile(n, target):
    t = min(n, target)
    assert n % t == 0, (n, t)
    return t


def _layer(x, pos0, past, w):
    b, t_len, d = x.shape
    n = b * t_len
    xf = x.reshape(n, d)

    tm = _tile(n, 512)
    tm_in = _tile(n, 256)
    assert tm % t_len == 0 or t_len % tm == 0
    assert tm_in % t_len == 0 or t_len % tm_in == 0
    pos = pos0 + jnp.arange(t_len, dtype=jnp.int32)
    tabs = []
    for rot in (MLA_ROPE, DSA_ROT, IDX_ROT):
        c, s = _rope_tables(pos, rot)
        if t_len < tm_in:
            c, s = jnp.tile(c, (tm_in // t_len, 1)), jnp.tile(s, (tm_in // t_len, 1))
        tabs += [c, s]

    (h, qfull, ckv, kpe, kpeb, qd, knew, vnew, knewb, vnewb, qi, ki, kib, wit) = _in_proj(
        xf, tabs, w, tm_in, t_len)

    if past is None:
        s_len = t_len
        tk = _tile(s_len, 512)
        s_pad = s_len
        all_ckv, all_kpeb, all_kb, all_vb, all_kib = ckv, kpeb, knewb, vnewb, kib
        conv_hist = jnp.zeros((b, CONV_W - 1, D_FF), F32)
    else:
        p_ckv, p_kpe, p_k, p_v, p_ki, conv_hist = past
        past_len = p_ckv.shape[1]
        s_len = past_len + t_len
        tk = 384 if s_len > 384 else _round_up(s_len, LANES)
        s_pad = _round_up(s_len, tk)
        padr = s_pad - s_len

        def cat(p_arr, new, width, dtype, lane_pad=0):
            p2 = p_arr.reshape(b, past_len, width).astype(dtype)
            if lane_pad:
                p2 = jnp.pad(p2, ((0, 0), (0, 0), (0, lane_pad)))
            a = jnp.concatenate([p2, new.reshape(b, t_len, width + lane_pad)], axis=1)
            return jnp.pad(a, ((0, 0), (0, padr), (0, 0))).reshape(b * s_pad, width + lane_pad)

        all_ckv = cat(p_ckv, ckv, KV_LORA, F32)
        all_kpeb = cat(p_kpe, kpeb, MLA_ROPE, BF16, LANES - MLA_ROPE)
        all_kb = cat(p_k, knewb, DSA_KV_HEADS * DSA_HD, BF16)
        all_vb = cat(p_v, vnewb, DSA_KV_HEADS * DSA_HD, BF16)
        all_kib = cat(p_ki, kib, IDX_DIM, BF16, LANES - IDX_DIM)

    kfull, vt_mla = _kv_up(all_ckv, all_kpeb, w, tk)
    tq = _tile(t_len, 256)
    n_ch = s_pad // tk
    vt_dsa = all_vb.reshape(b * n_ch, tk, DSA_KV_HEADS, DSA_HD).transpose(0, 2, 3, 1)
    o_mla = _mla_attn(qfull, kfull, vt_mla, b, t_len, s_pad, tq, tk, pos0, s_len, 4)
    topk = min(TOPK_MAX, s_len // 4)
    o_dsa = _dsa_attn(qi, wit, qd, all_kib, all_kb, vt_dsa, b, t_len, s_pad, tq, tk, pos0, s_len,
                      topk)

    x1, hf = _merge_out(h, o_mla, o_dsa, xf, w, tm, 512)

    t_seg = min(t_len, tm)
    assert t_seg & (t_seg - 1) == 0 and t_seg >= CONV_W - 1
    n_t = t_len // t_seg
    segs = tm // t_seg
    brows = hf.reshape(b, n_t, t_seg, d)[:, :, t_seg - (CONV_W - 1):, :].reshape(-1, d)
    bg = _gate_rows(brows, w['w_ffn_gate'], 512).reshape(b, n_t, CONV_W - 1, D_FF)
    halo = jnp.concatenate([conv_hist.astype(F32)[:, None], bg[:, :-1]], axis=1)
    halo = halo.reshape(n // tm, segs * (CONV_W - 1), D_FF)
    halo_rows = _round_up(halo.shape[1], 16)
    halo = jnp.pad(halo, ((0, 0), (0, halo_rows - halo.shape[1]), (0, 0)))
    y = _ffn(hf, x1, halo, w, tm, 512, t_seg)

    state = (ckv.reshape(b, t_len, KV_LORA), kpe.reshape(b, t_len, MLA_ROPE),
             knew.reshape(b, t_len, DSA_KV_HEADS, DSA_HD), vnew.reshape(b, t_len, DSA_KV_HEADS, DSA_HD),
             ki.reshape(b, t_len, IDX_DIM), bg[:, -1])
    return y.reshape(b, t_len, d), state


def kernel(x_prompt, x_sample, cache_mla_ckv, cache_mla_kpe, cache_dsa_k, cache_dsa_v, cache_idx_k, state_ffn_conv, attn_norm, w_in, q_a_norm, w_q_up, kv_a_norm, w_kv_up, mla_q_nope_norm, mla_q_rope_norm, mla_k_nope_norm, mla_k_rope_norm, dsa_q_norm, dsa_k_norm, w_o_mla, w_o_dsa, w_out, ffn_norm, w_ffn_up, conv_w, conv_b, w_ffn_down):
    depth = w_in.shape[0]
    past_len = cache_mla_ckv.shape[2]
    xp, xs = x_prompt, x_sample
    p_states, s_states = [], []
    for l in range(depth):
        w = _prep_weights(attn_norm[l], w_in[l], q_a_norm[l], w_q_up[l], kv_a_norm[l], w_kv_up[l],
                          mla_q_nope_norm[l], mla_q_rope_norm[l], mla_k_nope_norm[l],
                          mla_k_rope_norm[l], dsa_q_norm[l], dsa_k_norm[l], w_o_mla[l], w_o_dsa[l],
                          w_out[l], ffn_norm[l], w_ffn_up[l], conv_w[l], conv_b[l], w_ffn_down[l])
        past = (cache_mla_ckv[l], cache_mla_kpe[l], cache_dsa_k[l], cache_dsa_v[l],
                cache_idx_k[l], state_ffn_conv[l])
        xp, sp = _layer(xp, 0, None, w)
        xs, ss = _layer(xs, past_len, past, w)
        p_states.append(sp)
        s_states.append(ss)
    p_out = [jnp.stack(a) for a in zip(*p_states)]
    s_out = [jnp.stack(a) for a in zip(*s_states)]
    return (xp, xs, *p_out, *s_out)
```

```python
import functools

import jax
import jax.numpy as jnp
from jax import lax
from jax.experimental import pallas as pl
from jax.experimental.pallas import tpu as pltpu

CHUNK = 64
CHUNK_SHIFT = 6
ROPE_THETA = 500000.0
NORM_EPS = 1e-6
NEG_INF = -1e30
LOG2E = 1.4426950408889634
MLA_HEADS = 8
Q_LORA = 512
KV_LORA = 256
MLA_NOPE = 128
MLA_ROPE = 64
MLA_V = 128
MLA_QK_PAD = 256
DSA_HEADS = 8
DSA_KV_HEADS = 2
DSA_HD = 128
DSA_ROT = DSA_HD // 4
IDX_HEADS = 16
IDX_DIM = 64
IDX_ROT = IDX_DIM // 4
TOPK_MAX = 256
D_FF = 5632
CONV_W = 3
MLA_LOOKAHEAD = 6
DSA_LOOKAHEAD = 4
CHUNK_UNROLL = 2
COUNT_ROWS = 32

LANES = 128
SUBLANES = 8
VMEM_LIMIT = 56 * 1024 * 1024
INT_MIN = -2 ** 31

F32 = jnp.float32
BF16 = jnp.bfloat16


def _cparams(sem, flags=None):
    return pltpu.CompilerParams(dimension_semantics=sem, vmem_limit_bytes=VMEM_LIMIT, flags=flags)


def _resident(shape):
    nd = len(shape)
    return pl.BlockSpec(shape, lambda *_: (0,) * nd, pipeline_mode=pl.Buffered(1))


def _rms(x, g, n=None):
    n = x.shape[-1] if n is None else n
    ms = jnp.sum(x * x, axis=-1, keepdims=True) * (1.0 / n)
    return x * lax.rsqrt(ms + NORM_EPS) * g


def _rope128(x, c, s, half):
    lane = lax.broadcasted_iota(jnp.int32, x.shape, 1)
    swapped = jnp.where(lane < half, pltpu.roll(x, LANES - half, 1), pltpu.roll(x, half, 1))
    return x * c + swapped * s


def _dot(a, b):
    return jnp.dot(a, b, preferred_element_type=F32)


def _dot_t(a, b):
    return lax.dot_general(a, b, (((1,), (1,)), ((), ())), preferred_element_type=F32)


def _in_proj_kernel(x_ref, gattn_ref, wcq_ref, wsm_ref, wqd_ref, wkvd_ref, wqi_ref, wqup_ref,
                    gqa_ref, gkva_ref, gkpe_ref, gqn_ref, gqr_ref, gdq_ref, gdk_ref,
                    c64_ref, s64_ref, c32_ref, s32_ref, c16_ref, s16_ref,
                    h_ref, qfull_ref, ckv_ref, kpe_ref, kpeb_ref, qd_ref, knew_ref, vnew_ref,
                    knewb_ref, vnewb_ref, qi_ref, ki_ref, kib_ref, wit_ref):
    x = x_ref[...]
    hb = _rms(x, gattn_ref[...]).astype(BF16)
    h_ref[...] = hb
    c64, s64 = c64_ref[...], s64_ref[...]
    c32, s32 = c32_ref[...], s32_ref[...]
    c16, s16 = c16_ref[...], s16_ref[...]

    cqn = _rms(_dot(hb, wcq_ref[...]), gqa_ref[...]).astype(BF16)
    q = _dot(cqn, wqup_ref[...])
    scale = (MLA_NOPE + MLA_ROPE) ** -0.5 * LOG2E
    for hd in range(MLA_HEADS):
        lo = hd * MLA_QK_PAD
        nope = _rms(q[:, lo:lo + LANES], gqn_ref[...]) * scale
        qfull_ref[:, lo:lo + LANES] = nope.astype(BF16)
        pe = _rms(q[:, lo + LANES:lo + 2 * LANES], gqr_ref[...], MLA_ROPE)
        pe = _rope128(pe, c64, s64, MLA_ROPE // 2) * scale
        qfull_ref[:, lo + LANES:lo + 2 * LANES] = pe.astype(BF16)

    sm = _dot(hb, wsm_ref[...])
    ckv_ref[...] = _rms(sm[:, :KV_LORA], gkva_ref[...])
    kpe = _rope128(_rms(sm[:, KV_LORA:KV_LORA + LANES], gkpe_ref[...], MLA_ROPE), c64, s64,
                   MLA_ROPE // 2)
    kpe_ref[...] = kpe[:, :MLA_ROPE]
    kpeb_ref[...] = kpe.astype(BF16)
    ki = _rope128(sm[:, KV_LORA + LANES:KV_LORA + 2 * LANES], c16, s16, IDX_ROT // 2)
    ki_ref[...] = ki[:, :IDX_DIM]
    kib_ref[...] = ki.astype(BF16)
    wi = sm[:, KV_LORA + 2 * LANES:KV_LORA + 3 * LANES] * (IDX_HEADS ** -0.5)
    wit_ref[...] = wi.T[:IDX_HEADS]

    qd = _dot(hb, wqd_ref[...])
    for hd in range(DSA_HEADS):
        lo = hd * DSA_HD
        t = _rope128(_rms(qd[:, lo:lo + DSA_HD], gdq_ref[...]), c32, s32, DSA_ROT // 2)
        qd_ref[:, lo:lo + DSA_HD] = (t * (DSA_HD ** -0.5 * LOG2E)).astype(BF16)
    kvd = _dot(hb, wkvd_ref[...])
    for hd in range(DSA_KV_HEADS):
        lo = hd * DSA_HD
        t = _rope128(_rms(kvd[:, lo:lo + DSA_HD], gdk_ref[...]), c32, s32, DSA_ROT // 2)
        knew_ref[:, lo:lo + DSA_HD] = t
        knewb_ref[:, lo:lo + DSA_HD] = t.astype(BF16)
    v = kvd[:, DSA_KV_HEADS * DSA_HD:]
    vnew_ref[...] = v
    vnewb_ref[...] = v.astype(BF16)

    qi = _dot(hb, wqi_ref[...])
    for hd in range(IDX_HEADS):
        lo = hd * LANES
        t = _rope128(qi[:, lo:lo + LANES], c16, s16, IDX_ROT // 2)
        qi_ref[:, lo:lo + LANES] = (t * (IDX_DIM ** -0.5)).astype(BF16)


def _in_proj(xf, tabs, w, tm, t_len):
    n, d = xf.shape
    n_tiles = n // tm
    if t_len >= tm:
        per = t_len // tm
        tab_map = lambda i: (i % per, 0)
    else:
        tab_map = lambda i: (0, 0)
    row = lambda width: pl.BlockSpec((tm, width), lambda i: (i, 0))
    tab = pl.BlockSpec((tm, LANES), tab_map)
    weights = [w['attn_norm'], w['w_cq'], w['w_sm'], w['w_qd'], w['w_kvd'], w['w_qi'], w['w_qup'],
               w['q_a_norm'], w['kv_a_norm'], w['g_kpe'], w['g_qnope'], w['g_qrope'],
               w['dsa_q_norm'], w['dsa_k_norm']]
    out_widths = [(d, BF16), (MLA_HEADS * MLA_QK_PAD, BF16), (KV_LORA, F32), (MLA_ROPE, F32),
                  (LANES, BF16), (DSA_HEADS * DSA_HD, BF16), (DSA_KV_HEADS * DSA_HD, F32),
                  (DSA_KV_HEADS * DSA_HD, F32), (DSA_KV_HEADS * DSA_HD, BF16),
                  (DSA_KV_HEADS * DSA_HD, BF16), (IDX_HEADS * LANES, BF16), (IDX_DIM, F32),
                  (LANES, BF16)]
    return pl.pallas_call(
        _in_proj_kernel,
        grid=(n_tiles,),
        in_specs=[row(d)] + [_resident(a.shape) for a in weights] + [tab] * 6,
        out_specs=[row(wd) for wd, _ in out_widths]
        + [pl.BlockSpec((IDX_HEADS, tm), lambda i: (0, i))],
        out_shape=[jax.ShapeDtypeStruct((n, wd), dt) for wd, dt in out_widths]
        + [jax.ShapeDtypeStruct((IDX_HEADS, n), F32)],
        compiler_params=_cparams(("parallel",)),
        name="in_proj",
    )(xf, *weights, *tabs)


def _kv_up_kernel(ckv_ref, kpe_ref, w_ref, g_ref, kfull_ref, vt_ref):
    kv = _dot(ckv_ref[...].astype(BF16), w_ref[...])
    kpe = kpe_ref[...]
    g = g_ref[...]
    for hd in range(MLA_HEADS):
        lo = hd * MLA_QK_PAD
        kfull_ref[:, lo:lo + LANES] = _rms(kv[:, hd * LANES:(hd + 1) * LANES], g).astype(BF16)
        kfull_ref[:, lo + LANES:lo + 2 * LANES] = kpe
    for hd in range(MLA_HEADS):
        lo = MLA_HEADS * MLA_NOPE + hd * MLA_V
        vt_ref[0, hd] = kv[:, lo:lo + MLA_V].T.astype(BF16)


def _kv_up(ckv, kpeb, w, tm):
    n = ckv.shape[0]
    row = lambda width: pl.BlockSpec((tm, width), lambda i: (i, 0))
    return pl.pallas_call(
        _kv_up_kernel,
        grid=(n // tm,),
        in_specs=[row(KV_LORA), row(LANES), _resident(w['w_kvup'].shape),
                  _resident(w['g_knope'].shape)],
        out_specs=[row(MLA_HEADS * MLA_QK_PAD),
                   pl.BlockSpec((1, MLA_HEADS, MLA_V, tm), lambda i: (i, 0, 0, 0))],
        out_shape=[jax.ShapeDtypeStruct((n, MLA_HEADS * MLA_QK_PAD), BF16),
                   jax.ShapeDtypeStruct((n // tm, MLA_HEADS, MLA_V, tm), BF16)],
        compiler_params=_cparams(("parallel",)),
        name="kv_up",
    )(ckv, kpeb, w['w_kvup'], w['g_knope'])


def _visible_limit(q_last_pos, s_valid):
    return jnp.minimum(((q_last_pos >> CHUNK_SHIFT) + 1) * CHUNK, s_valid)


def _softmax_step(s, vt, m, l, acc):
    m_new = jnp.maximum(m, jnp.max(s, axis=0, keepdims=True))
    alpha = jnp.exp2(m - m_new)
    p = jnp.exp2(s - m_new)
    l = alpha * l + jnp.sum(p, axis=0, keepdims=True)
    acc = alpha * acc + _dot(vt, p.astype(BF16))
    return m_new, l, acc


def _run_tiles(tiles, score_fn, value_fn, carry, lookahead):
    carry = list(carry)
    pending = [score_fn(*t) for t in tiles[:lookahead]]
    for i, (stream, chunk) in enumerate(tiles):
        s = pending.pop(0)
        carry[stream] = _softmax_step(s, value_fn(stream, chunk), *carry[stream])
        if i + lookahead < len(tiles):
            pending.append(score_fn(*tiles[i + lookahead]))
    return tuple(carry)


def _softmax_init(tq, dv, streams):
    return tuple((jnp.full((1, tq), NEG_INF, F32), jnp.zeros((1, tq), F32),
                  jnp.zeros((dv, tq), F32)) for _ in range(streams))


def _lane_width(tq):
    return max(tq, LANES)


def _transposed(block):
    x = block.astype(F32)
    tq, d = x.shape
    if tq < _lane_width(tq):
        x = jnp.concatenate([x, jnp.zeros((_lane_width(tq) - tq, d), F32)], axis=0)
    return x.T.astype(BF16)


def _visibility(start, tk, qchunk, s_valid):
    kpos = start + lax.broadcasted_iota(jnp.int32, (tk, 1), 0)
    return ((kpos >> CHUNK_SHIFT) <= qchunk) & (kpos < s_valid)


def _mla_kernel(q_ref, k_ref, vt_ref, o_ref, *, tq, tk, pos0, s_valid, nh):
    qt = pl.program_id(2)
    q_first = pos0 + qt * tq
    tql = _lane_width(tq)
    qchunk = (q_first + lax.broadcasted_iota(jnp.int32, (1, tql), 1)) >> CHUNK_SHIFT
    n_chunks = pl.cdiv(_visible_limit(q_first + tq - 1, s_valid), tk)
    n_full = _visible_limit(q_first, s_valid) // tk
    qts = [_transposed(q_ref[:, hd * MLA_QK_PAD:(hd + 1) * MLA_QK_PAD]) for hd in range(nh)]

    def make_body(masked, unroll):
        def body(cu, carry):
            def score(hd, u):
                start = pl.multiple_of((cu * unroll + u) * tk, tk)
                s = _dot(k_ref[pl.ds(start, tk), hd * MLA_QK_PAD:(hd + 1) * MLA_QK_PAD], qts[hd])
                if masked:
                    s = jnp.where(_visibility(start, tk, qchunk, s_valid), s, NEG_INF)
                return s

            def value(hd, u):
                return vt_ref[cu * unroll + u, hd]

            tiles = [(hd, u) for u in range(unroll) for hd in range(nh)]
            return _run_tiles(tiles, score, value, carry, MLA_LOOKAHEAD)
        return body

    n_pairs = n_full // CHUNK_UNROLL
    carry = lax.fori_loop(0, n_pairs, make_body(False, CHUNK_UNROLL),
                          _softmax_init(tql, MLA_V, nh))
    carry = lax.fori_loop(n_pairs * CHUNK_UNROLL, n_chunks, make_body(True, 1), carry)
    for hd in range(nh):
        _, l, acc = carry[hd]
        o_ref[:, hd * MLA_V:(hd + 1) * MLA_V] = (acc / l).T[:tq].astype(BF16)


def _mla_attn(qfull, kfull, vt, b, t_len, s_pad, tq, tk, pos0, s_valid, nh):
    nq = t_len // tq
    n_ch = s_pad // tk
    kern = functools.partial(_mla_kernel, tq=tq, tk=tk, pos0=pos0, s_valid=s_valid, nh=nh)
    return pl.pallas_call(
        kern,
        grid=(b, MLA_HEADS // nh, nq),
        in_specs=[pl.BlockSpec((tq, nh * MLA_QK_PAD), lambda bi, h, i: (bi * nq + i, h)),
                  pl.BlockSpec((s_pad, nh * MLA_QK_PAD), lambda bi, h, i: (bi, h)),
                  pl.BlockSpec((n_ch, nh, MLA_V, tk), lambda bi, h, i: (bi, h, 0, 0))],
        out_specs=pl.BlockSpec((tq, nh * MLA_V), lambda bi, h, i: (bi * nq + i, h)),
        out_shape=jax.ShapeDtypeStruct((b * t_len, MLA_HEADS * MLA_V), BF16),
        compiler_params=_cparams(("parallel", "parallel", "arbitrary")),
        name="mla_attn",
    )(qfull, kfull, vt)


def _count_keys(preds, key_scr, n_chunks, tq, tk):
    def make_body(unroll):
        def body(cu, parts):
            parts = list(parts)
            for u in range(unroll):
                c = cu * unroll + u
                keys = key_scr[c]
                idx = c * tk + lax.broadcasted_iota(jnp.int32, (tk, 1), 0)
                for n, pred in enumerate(preds):
                    hit = jnp.where(pred(keys, idx), 1, 0).reshape(tk // COUNT_ROWS, COUNT_ROWS, tq)
                    parts[n] = parts[n] + jnp.sum(hit, axis=0)
            return tuple(parts)
        return body
    zero = jnp.zeros((COUNT_ROWS, tq), jnp.int32)
    n_pairs = n_chunks // CHUNK_UNROLL
    parts = lax.fori_loop(0, n_pairs, make_body(CHUNK_UNROLL), (zero,) * len(preds))
    parts = lax.fori_loop(n_pairs * CHUNK_UNROLL, n_chunks, make_body(1), parts)
    return [jnp.sum(p, axis=0, keepdims=True) for p in parts]


def _dsa_kernel(qi_ref, wit_ref, qd_ref, ki_ref, k_ref, vt_ref, o_ref, key_scr, bias_scr, cut_scr,
                *, tq, tk, streams, pos0, s_valid, topk, idx_bits):
    qt = pl.program_id(1)
    q_first = pos0 + qt * tq
    tql = _lane_width(tq)
    lane = lax.broadcasted_iota(jnp.int32, (1, tql), 1)
    qchunk = (q_first + lane) >> CHUNK_SHIFT
    n_chunks = pl.cdiv(_visible_limit(q_first + tq - 1, s_valid), tk)
    wit = wit_ref[0]
    if tq < tql:
        wit = jnp.concatenate([wit, jnp.zeros((IDX_HEADS, tql - tq), F32)], axis=1)
    qits = [_transposed(qi_ref[:, hd * LANES:(hd + 1) * LANES]) for hd in range(IDX_HEADS)]

    def score_body(c, _):
        start = pl.multiple_of(c * tk, tk)
        kic = ki_ref[pl.ds(start, tk), :]
        sc = jnp.zeros((tk, tql), F32)
        for hd in range(IDX_HEADS):
            sc = sc + wit[hd:hd + 1, :] * jnp.maximum(_dot(kic, qits[hd]), 0.0)
        bits = pltpu.bitcast(sc + 0.0, jnp.int32)
        keys = bits ^ ((bits >> 31) & 0x7FFFFFFF)
        key_scr[c] = jnp.where(_visibility(start, tk, qchunk, s_valid), keys, INT_MIN)
        return 0
    lax.fori_loop(0, n_chunks, score_body, 0)

    def bit_body(b, t):
        cand = t + (jnp.int32(1) << (31 - b))
        cnt, = _count_keys([lambda kk, _: kk >= cand], key_scr, n_chunks, tql, tk)
        return jnp.where(cnt >= topk, cand, t)
    thr = lax.fori_loop(0, 32, bit_body, jnp.full((1, tql), INT_MIN, jnp.int32))

    cnt_gt, cnt_eq = _count_keys([lambda kk, _: kk > thr, lambda kk, _: kk == thr],
                                 key_scr, n_chunks, tql, tk)
    need = topk - cnt_gt
    cut_scr[...] = jnp.full((SUBLANES, tql), 2 ** 30, jnp.int32)
    excess = jnp.max(jnp.where((cnt_eq > need) & (thr > INT_MIN) & (lane < tq), 1, 0))

    @pl.when(excess > 0)
    def _():
        def idx_body(b, pos):
            cand = pos + (jnp.int32(1) << (idx_bits - 1 - b))
            cnt, = _count_keys([lambda kk, idx: (kk == thr) & (idx < cand)], key_scr, n_chunks,
                               tql, tk)
            return jnp.where(cnt < need, cand, pos)
        pos = lax.fori_loop(0, idx_bits, idx_body, jnp.zeros((1, tql), jnp.int32))
        cut_scr[...] = jnp.broadcast_to(pos, (SUBLANES, tql))
    cut = jnp.where(thr > INT_MIN, cut_scr[0:1, :], -1)

    def bias_body(c, _):
        keys = key_scr[c]
        idx = c * tk + lax.broadcasted_iota(jnp.int32, (tk, 1), 0)
        tie = jnp.where(idx <= cut, 0.0, NEG_INF)
        bias_scr[c] = jnp.where(keys > thr, 0.0, jnp.where(keys == thr, tie, NEG_INF))
        return 0
    lax.fori_loop(0, n_chunks, bias_body, 0)

    rep = DSA_HEADS // DSA_KV_HEADS
    for h0 in range(0, DSA_HEADS, streams):
        qts = [_transposed(qd_ref[:, hd * DSA_HD:(hd + 1) * DSA_HD]) for hd in range(h0, h0 + streams)]

        def make_att_body(unroll, qts=qts, h0=h0):
            def att_body(cu, carry):
                def score(n, u):
                    c = cu * unroll + u
                    start = pl.multiple_of(c * tk, tk)
                    g = (h0 + n) // rep
                    kc = k_ref[pl.ds(start, tk), g * DSA_HD:(g + 1) * DSA_HD]
                    return _dot(kc, qts[n]) + bias_scr[c]

                def value(n, u):
                    return vt_ref[cu * unroll + u, (h0 + n) // rep]

                tiles = [(n, u) for u in range(unroll) for n in range(streams)]
                return _run_tiles(tiles, score, value, carry, DSA_LOOKAHEAD)
            return att_body

        n_pairs = n_chunks // CHUNK_UNROLL
        carry = lax.fori_loop(0, n_pairs, make_att_body(CHUNK_UNROLL),
                              _softmax_init(tql, DSA_HD, streams))
        carry = lax.fori_loop(n_pairs * CHUNK_UNROLL, n_chunks, make_att_body(1), carry)
        for n, (_, l, acc) in enumerate(carry):
            hd = h0 + n
            o_ref[:, hd * DSA_HD:(hd + 1) * DSA_HD] = (acc / l).T[:tq].astype(BF16)


def _dsa_attn(qi, wit, qd, kib, kb, vt, b, t_len, s_pad, tq, tk, pos0, s_valid, topk):
    nq = t_len // tq
    n_ch = s_pad // tk
    idx_bits = max(1, (s_pad - 1).bit_length())
    tql = _lane_width(tq)
    wit = wit.reshape(IDX_HEADS, b * nq, tq).transpose(1, 0, 2)
    kern = functools.partial(_dsa_kernel, tq=tq, tk=tk, streams=8, pos0=pos0, s_valid=s_valid,
                             topk=topk, idx_bits=idx_bits)
    qrow = lambda width: pl.BlockSpec((tq, width), lambda bi, i: (bi * nq + i, 0))
    krow = lambda width: pl.BlockSpec((s_pad, width), lambda bi, i: (bi, 0))
    return pl.pallas_call(
        kern,
        grid=(b, nq),
        in_specs=[qrow(IDX_HEADS * LANES),
                  pl.BlockSpec((1, IDX_HEADS, tq), lambda bi, i: (bi * nq + i, 0, 0)),
                  qrow(DSA_HEADS * DSA_HD), krow(LANES), krow(DSA_KV_HEADS * DSA_HD),
                  pl.BlockSpec((n_ch, DSA_KV_HEADS, DSA_HD, tk), lambda bi, i: (bi, 0, 0, 0))],
        out_specs=qrow(DSA_HEADS * DSA_HD),
        out_shape=jax.ShapeDtypeStruct((b * t_len, DSA_HEADS * DSA_HD), BF16),
        scratch_shapes=[pltpu.VMEM((n_ch, tk, tql), jnp.int32), pltpu.VMEM((n_ch, tk, tql), F32),
                        pltpu.VMEM((SUBLANES, tql), jnp.int32)],
        compiler_params=_cparams(("parallel", "arbitrary")),
        name="dsa_attn",
    )(qi, wit, qd, kib, kb, vt)


def _sigmoid(x):
    return 1.0 / (1.0 + jnp.exp(-x))


def _merge_out_kernel(h_ref, om_ref, od_ref, x_ref, wgm_ref, wgd_ref, wom_ref, wod_ref, wout_ref,
                      gffn_ref, x1_ref, hf_ref, acc_scr):
    j = pl.program_id(1)

    @pl.when(j == 0)
    def _():
        acc_scr[...] = x_ref[...]

    h = h_ref[...]
    merged = (_sigmoid(_dot(h, wgm_ref[...])) * _dot(om_ref[...], wom_ref[...])
              + _sigmoid(_dot(h, wgd_ref[...])) * _dot(od_ref[...], wod_ref[...]))
    acc_scr[...] += _dot(merged.astype(BF16), wout_ref[...])

    @pl.when(j == pl.num_programs(1) - 1)
    def _():
        x1 = acc_scr[...]
        x1_ref[...] = x1
        hf_ref[...] = _rms(x1, gffn_ref[...]).astype(BF16)


def _merge_out(h, om, od, xf, w, tm, tn):
    n, d = xf.shape
    row = lambda width: pl.BlockSpec((tm, width), lambda i, j: (i, 0))
    col = lambda rows: pl.BlockSpec((rows, tn), lambda i, j: (0, j))
    return pl.pallas_call(
        _merge_out_kernel,
        grid=(n // tm, d // tn),
        in_specs=[row(d), row(om.shape[1]), row(od.shape[1]), row(d),
                  col(d), col(d), col(om.shape[1]), col(od.shape[1]),
                  pl.BlockSpec((tn, d), lambda i, j: (j, 0)),
                  pl.BlockSpec((1, d), lambda i, j: (0, 0))],
        out_specs=[row(d), row(d)],
        out_shape=[jax.ShapeDtypeStruct((n, d), F32), jax.ShapeDtypeStruct((n, d), BF16)],
        scratch_shapes=[pltpu.VMEM((tm, d), F32)],
        compiler_params=_cparams(("parallel", "arbitrary")),
        name="merge_out",
    )(h, om, od, xf, w['w_gm'], w['w_gd'], w['w_om'], w['w_od'], w['w_out'], w['ffn_norm'])


def _gate_rows_kernel(h_ref, w_ref, o_ref):
    o_ref[...] = _dot(h_ref[...], w_ref[...])


def _gate_rows(rows, w_gate, tf):
    r, d = rows.shape
    return pl.pallas_call(
        _gate_rows_kernel,
        grid=(D_FF // tf,),
        in_specs=[pl.BlockSpec((r, d), lambda j: (0, 0)), pl.BlockSpec((d, tf), lambda j: (0, j))],
        out_specs=pl.BlockSpec((r, tf), lambda j: (0, j)),
        out_shape=jax.ShapeDtypeStruct((r, D_FF), F32),
        compiler_params=_cparams(("parallel",)),
        name="ffn_gate_rows",
    )(rows, w_gate)


def _split3(x):
    hi = x.astype(BF16)
    r1 = x - hi.astype(F32)
    mid = r1.astype(BF16)
    lo = (r1 - mid.astype(F32)).astype(BF16)
    return hi, mid, lo


def _ffn_kernel(hf_ref, x1_ref, halo_ref, wg_ref, wu_ref, wd_ref, cw_ref, cb_ref, y_ref, acc_scr,
                *, tm, t_seg, halo_rows):
    j = pl.program_id(1)

    @pl.when(j == 0)
    def _():
        acc_scr[...] = x1_ref[...]

    hf = hf_ref[...]

    r1 = lax.broadcasted_iota(jnp.int32, (tm, 1), 0) & (t_seg - 1)
    halo = halo_ref[0]
    if t_seg == tm:
        fix1 = halo[1:2, :]
        fix2 = jnp.where(r1 < 1, halo[0:1, :], halo[1:2, :])
    else:
        seg_shift = t_seg.bit_length() - 1
        row = lax.broadcasted_iota(jnp.int32, (tm, halo_rows), 0)
        colh = lax.broadcasted_iota(jnp.int32, (tm, halo_rows), 1)
        seg2 = (row >> seg_shift) * 2
        r_in = row & (t_seg - 1)
        e1 = jnp.where((r_in == 0) & (colh == seg2 + 1), 1.0, 0.0).astype(BF16)
        e2 = jnp.where(((r_in == 0) & (colh == seg2)) | ((r_in == 1) & (colh == seg2 + 1)),
                       1.0, 0.0).astype(BF16)
        pieces = _split3(halo)
        fix1 = _dot(e1, pieces[0]) + _dot(e1, pieces[1]) + _dot(e1, pieces[2])
        fix2 = _dot(e2, pieces[0]) + _dot(e2, pieces[1]) + _dot(e2, pieces[2])
    cw = cw_ref[...]
    cb = cb_ref[...]

    g = _dot(hf, wg_ref[...])
    up = _dot(hf, wu_ref[...])
    g1 = jnp.where(r1 < 1, fix1, pltpu.roll(g, 1, 0))
    g2 = jnp.where(r1 < 2, fix2, pltpu.roll(g, 2, 0))
    conv = cb + cw[0:1, :] * g2 + cw[1:2, :] * g1 + cw[2:3, :] * g
    act = conv * _sigmoid(conv) * up
    acc_scr[...] += _dot(act.astype(BF16), wd_ref[...])

    @pl.when(j == pl.num_programs(1) - 1)
    def _():
        y_ref[...] = acc_scr[...]


def _ffn(hf, x1, halo, w, tm, tf, t_seg):
    n, d = x1.shape
    halo_rows = halo.shape[1]
    kern = functools.partial(_ffn_kernel, tm=tm, t_seg=t_seg, halo_rows=halo_rows)
    row = lambda width: pl.BlockSpec((tm, width), lambda i, j: (i, 0))
    return pl.pallas_call(
        kern,
        grid=(n // tm, D_FF // tf),
        in_specs=[row(d), row(d),
                  pl.BlockSpec((1, halo_rows, tf), lambda i, j: (i, 0, j)),
                  pl.BlockSpec((d, tf), lambda i, j: (0, j)),
                  pl.BlockSpec((d, tf), lambda i, j: (0, j)),
                  pl.BlockSpec((tf, d), lambda i, j: (j, 0)),
                  pl.BlockSpec((CONV_W, tf), lambda i, j: (0, j)),
                  pl.BlockSpec((1, tf), lambda i, j: (0, j))],
        out_specs=row(d),
        out_shape=jax.ShapeDtypeStruct((n, d), F32),
        scratch_shapes=[pltpu.VMEM((tm, d), F32)],
        compiler_params=_cparams(("parallel", "arbitrary")),
        name="conv_ffn",
    )(hf, x1, halo, w['w_ffn_gate'], w['w_ffn_upv'], w['w_ffn_down'], w['conv_w'], w['conv_b'])


def _pad_cols(a, width):
    return jnp.pad(a, ((0, 0), (0, width - a.shape[1])))


def _prep_weights(attn_norm, w_in, q_a_norm, w_q_up, kv_a_norm, w_kv_up, mla_q_nope_norm,
                  mla_q_rope_norm, mla_k_nope_norm, mla_k_rope_norm, dsa_q_norm, dsa_k_norm,
                  w_o_mla, w_o_dsa, w_out, ffn_norm, w_ffn_up, conv_w, conv_b, w_ffn_down):
    d = w_in.shape[0]
    widths = (Q_LORA, KV_LORA + MLA_ROPE, DSA_HEADS * DSA_HD, 2 * DSA_KV_HEADS * DSA_HD,
              IDX_HEADS * IDX_DIM, IDX_DIM, IDX_HEADS, 2 * d)
    offs = [0]
    for wd in widths:
        offs.append(offs[-1] + wd)
    w_in = w_in.astype(BF16)
    seg = [w_in[:, offs[i]:offs[i + 1]] for i in range(len(widths))]
    w = {}
    w['w_cq'] = seg[0]
    w['w_sm'] = jnp.concatenate(
        [seg[1][:, :KV_LORA], _pad_cols(seg[1][:, KV_LORA:], LANES), _pad_cols(seg[5], LANES),
         _pad_cols(seg[6], LANES)], axis=1)
    w['w_qd'] = seg[2]
    w['w_kvd'] = seg[3]
    w['w_qi'] = jnp.pad(seg[4].reshape(d, IDX_HEADS, IDX_DIM),
                        ((0, 0), (0, 0), (0, LANES - IDX_DIM))).reshape(d, IDX_HEADS * LANES)
    w['w_gm'] = seg[7][:, :d]
    w['w_gd'] = seg[7][:, d:]
    qup = w_q_up.astype(BF16).reshape(Q_LORA, MLA_HEADS, MLA_NOPE + MLA_ROPE)
    w['w_qup'] = jnp.pad(qup, ((0, 0), (0, 0), (0, MLA_QK_PAD - MLA_NOPE - MLA_ROPE))
                         ).reshape(Q_LORA, MLA_HEADS * MLA_QK_PAD)
    kvup = w_kv_up.astype(BF16).reshape(KV_LORA, MLA_HEADS, MLA_NOPE + MLA_V)
    w['w_kvup'] = jnp.concatenate(
        [kvup[:, :, :MLA_NOPE].reshape(KV_LORA, MLA_HEADS * MLA_NOPE),
         kvup[:, :, MLA_NOPE:].reshape(KV_LORA, MLA_HEADS * MLA_V)], axis=1)
    r2 = lambda a: a.reshape(1, -1).astype(F32)
    w['attn_norm'] = r2(attn_norm)
    w['q_a_norm'] = r2(q_a_norm)
    w['kv_a_norm'] = r2(kv_a_norm)
    w['g_kpe'] = _pad_cols(r2(mla_k_rope_norm), LANES)
    w['g_qnope'] = r2(mla_q_nope_norm)
    w['g_qrope'] = _pad_cols(r2(mla_q_rope_norm), LANES)
    w['g_knope'] = r2(mla_k_nope_norm)
    w['dsa_q_norm'] = r2(dsa_q_norm)
    w['dsa_k_norm'] = r2(dsa_k_norm)
    w['ffn_norm'] = r2(ffn_norm)
    w['w_om'] = w_o_mla.astype(BF16)
    w['w_od'] = w_o_dsa.astype(BF16)
    w['w_out'] = w_out.astype(BF16)
    w['w_ffn_gate'] = w_ffn_up[:, :D_FF].astype(BF16)
    w['w_ffn_upv'] = w_ffn_up[:, D_FF:].astype(BF16)
    w['w_ffn_down'] = w_ffn_down.astype(BF16)
    w['conv_w'] = conv_w.astype(F32)
    w['conv_b'] = r2(conv_b)
    return w


def _rope_tables(pos, rot):
    half = rot // 2
    inv = ROPE_THETA ** (-jnp.arange(half, dtype=F32) / half)
    ang = pos.astype(F32)[:, None] * inv[None, :]
    cos, sin = jnp.cos(ang), jnp.sin(ang)
    t = pos.shape[0]
    c = jnp.concatenate([cos, cos, jnp.ones((t, LANES - rot), F32)], axis=1)
    s = jnp.concatenate([-sin, sin, jnp.zeros((t, LANES - rot), F32)], axis=1)
    return c, s


def _round_up(a, m):
    return (a + m - 1) // m * m


def _tile(n, target):
    t = min(n, target)
    assert n % t == 0, (n, t)
    return t


def _layer(x, pos0, past, w):
    b, t_len, d = x.shape
    n = b * t_len
    xf = x.reshape(n, d)

    tm = _tile(n, 512)
    tm_in = _tile(n, 256)
    assert tm % t_len == 0 or t_len % tm == 0
    assert tm_in % t_len == 0 or t_len % tm_in == 0
    pos = pos0 + jnp.arange(t_len, dtype=jnp.int32)
    tabs = []
    for rot in (MLA_ROPE, DSA_ROT, IDX_ROT):
        c, s = _rope_tables(pos, rot)
        if t_len < tm_in:
            c, s = jnp.tile(c, (tm_in // t_len, 1)), jnp.tile(s, (tm_in // t_len, 1))
        tabs += [c, s]

    (h, qfull, ckv, kpe, kpeb, qd, knew, vnew, knewb, vnewb, qi, ki, kib, wit) = _in_proj(
        xf, tabs, w, tm_in, t_len)

    if past is None:
        s_len = t_len
        tk = _tile(s_len, 512)
        s_pad = s_len
        all_ckv, all_kpeb, all_kb, all_vb, all_kib = ckv, kpeb, knewb, vnewb, kib
        conv_hist = jnp.zeros((b, CONV_W - 1, D_FF), F32)
    else:
        p_ckv, p_kpe, p_k, p_v, p_ki, conv_hist = past
        past_len = p_ckv.shape[1]
        s_len = past_len + t_len
        tk = 384 if s_len > 384 else _round_up(s_len, LANES)
        s_pad = _round_up(s_len, tk)
        padr = s_pad - s_len

        def cat(p_arr, new, width, dtype, lane_pad=0):
            p2 = p_arr.reshape(b, past_len, width).astype(dtype)
            if lane_pad:
                p2 = jnp.pad(p2, ((0, 0), (0, 0), (0, lane_pad)))
            a = jnp.concatenate([p2, new.reshape(b, t_len, width + lane_pad)], axis=1)
            return jnp.pad(a, ((0, 0), (0, padr), (0, 0))).reshape(b * s_pad, width + lane_pad)

        all_ckv = cat(p_ckv, ckv, KV_LORA, F32)
        all_kpeb = cat(p_kpe, kpeb, MLA_ROPE, BF16, LANES - MLA_ROPE)
        all_kb = cat(p_k, knewb, DSA_KV_HEADS * DSA_HD, BF16)
        all_vb = cat(p_v, vnewb, DSA_KV_HEADS * DSA_HD, BF16)
        all_kib = cat(p_ki, kib, IDX_DIM, BF16, LANES - IDX_DIM)

    kfull, vt_mla = _kv_up(all_ckv, all_kpeb, w, tk)
    tq = _tile(t_len, 256)
    n_ch = s_pad // tk
    vt_dsa = all_vb.reshape(b * n_ch, tk, DSA_KV_HEADS, DSA_HD).transpose(0, 2, 3, 1)
    o_mla = _mla_attn(qfull, kfull, vt_mla, b, t_len, s_pad, _tile(t_len, 512), tk, pos0, s_len, 4)
    topk = min(TOPK_MAX, s_len // 4)
    o_dsa = _dsa_attn(qi, wit, qd, all_kib, all_kb, vt_dsa, b, t_len, s_pad, _tile(t_len, 512), tk,
                      pos0, s_len, topk)

    x1, hf = _merge_out(h, o_mla, o_dsa, xf, w, tm, 512)

    t_seg = min(t_len, tm)
    assert t_seg & (t_seg - 1) == 0 and t_seg >= CONV_W - 1
    n_t = t_len // t_seg
    segs = tm // t_seg
    brows = hf.reshape(b, n_t, t_seg, d)[:, :, t_seg - (CONV_W - 1):, :].reshape(-1, d)
    bg = _gate_rows(brows, w['w_ffn_gate'], 512).reshape(b, n_t, CONV_W - 1, D_FF)
    halo = jnp.concatenate([conv_hist.astype(F32)[:, None], bg[:, :-1]], axis=1)
    halo = halo.reshape(n // tm, segs * (CONV_W - 1), D_FF)
    halo_rows = _round_up(halo.shape[1], 16)
    halo = jnp.pad(halo, ((0, 0), (0, halo_rows - halo.shape[1]), (0, 0)))
    y = _ffn(hf, x1, halo, w, tm, 512, t_seg)

    state = (ckv.reshape(b, t_len, KV_LORA), kpe.reshape(b, t_len, MLA_ROPE),
             knew.reshape(b, t_len, DSA_KV_HEADS, DSA_HD), vnew.reshape(b, t_len, DSA_KV_HEADS, DSA_HD),
             ki.reshape(b, t_len, IDX_DIM), bg[:, -1])
    return y.reshape(b, t_len, d), state


def kernel(x_prompt, x_sample, cache_mla_ckv, cache_mla_kpe, cache_dsa_k, cache_dsa_v, cache_idx_k, state_ffn_conv, attn_norm, w_in, q_a_norm, w_q_up, kv_a_norm, w_kv_up, mla_q_nope_norm, mla_q_rope_norm, mla_k_nope_norm, mla_k_rope_norm, dsa_q_norm, dsa_k_norm, w_o_mla, w_o_dsa, w_out, ffn_norm, w_ffn_up, conv_w, conv_b, w_ffn_down):
    depth = w_in.shape[0]
    past_len = cache_mla_ckv.shape[2]
    xp, xs = x_prompt, x_sample
    p_states, s_states = [], []
    for l in range(depth):
        w = _prep_weights(attn_norm[l], w_in[l], q_a_norm[l], w_q_up[l], kv_a_norm[l], w_kv_up[l],
                          mla_q_nope_norm[l], mla_q_rope_norm[l], mla_k_nope_norm[l],
                          mla_k_rope_norm[l], dsa_q_norm[l], dsa_k_norm[l], w_o_mla[l], w_o_dsa[l],
                          w_out[l], ffn_norm[l], w_ffn_up[l], conv_w[l], conv_b[l], w_ffn_down[l])
        past = (cache_mla_ckv[l], cache_mla_kpe[l], cache_dsa_k[l], cache_dsa_v[l],
                cache_idx_k[l], state_ffn_conv[l])
        xp, sp = _layer(xp, 0, None, w)
        xs, ss = _layer(xs, past_len, past, w)
        p_states.append(sp)
        s_states.append(ss)
    p_out = [jnp.stack(a) for a in zip(*p_states)]
    s_out = [jnp.stack(a) for a in zip(*s_states)]
    return (xp, xs, *p_out, *s_out)
```

```python
import functools

import jax
import jax.numpy as jnp
from jax import lax
from jax.experimental import pallas as pl
from jax.experimental.pallas import tpu as pltpu

CHUNK = 64
CHUNK_SHIFT = 6
ROPE_THETA = 500000.0
NORM_EPS = 1e-6
NEG_INF = -1e30
LOG2E = 1.4426950408889634
MLA_HEADS = 8
Q_LORA = 512
KV_LORA = 256
MLA_NOPE = 128
MLA_ROPE = 64
MLA_V = 128
MLA_QK_PAD = 256
DSA_HEADS = 8
DSA_KV_HEADS = 2
DSA_HD = 128
DSA_ROT = DSA_HD // 4
IDX_HEADS = 16
IDX_DIM = 64
IDX_ROT = IDX_DIM // 4
TOPK_MAX = 256
D_FF = 5632
CONV_W = 3
MLA_LOOKAHEAD = 6
DSA_LOOKAHEAD = 4
CHUNK_UNROLL = 2
COUNT_ROWS = 32

LANES = 128
SUBLANES = 8
VMEM_LIMIT = 56 * 1024 * 1024
INT_MIN = -2 ** 31

F32 = jnp.float32
BF16 = jnp.bfloat16


def _cparams(sem, flags=None):
    return pltpu.CompilerParams(dimension_semantics=sem, vmem_limit_bytes=VMEM_LIMIT, flags=flags)


def _resident(shape):
    nd = len(shape)
    return pl.BlockSpec(shape, lambda *_: (0,) * nd, pipeline_mode=pl.Buffered(1))


def _rms(x, g, n=None):
    n = x.shape[-1] if n is None else n
    ms = jnp.sum(x * x, axis=-1, keepdims=True) * (1.0 / n)
    return x * lax.rsqrt(ms + NORM_EPS) * g


def _rope128(x, c, s, half):
    lane = lax.broadcasted_iota(jnp.int32, x.shape, 1)
    swapped = jnp.where(lane < half, pltpu.roll(x, LANES - half, 1), pltpu.roll(x, half, 1))
    return x * c + swapped * s


def _dot(a, b):
    return jnp.dot(a, b, preferred_element_type=F32)


def _dot_t(a, b):
    return lax.dot_general(a, b, (((1,), (1,)), ((), ())), preferred_element_type=F32)


def _in_proj_kernel(x_ref, gattn_ref, wcq_ref, wsm_ref, wqd_ref, wkvd_ref, wqi_ref, wqup_ref,
                    gqa_ref, gkva_ref, gkpe_ref, gqn_ref, gqr_ref, gdq_ref, gdk_ref,
                    c64_ref, s64_ref, c32_ref, s32_ref, c16_ref, s16_ref,
                    h_ref, qfull_ref, ckv_ref, kpe_ref, kpeb_ref, qd_ref, knew_ref, vnew_ref,
                    knewb_ref, vnewb_ref, qi_ref, ki_ref, kib_ref, wit_ref):
    x = x_ref[...]
    hb = _rms(x, gattn_ref[...]).astype(BF16)
    h_ref[...] = hb
    c64, s64 = c64_ref[...], s64_ref[...]
    c32, s32 = c32_ref[...], s32_ref[...]
    c16, s16 = c16_ref[...], s16_ref[...]

    cqn = _rms(_dot(hb, wcq_ref[...]), gqa_ref[...]).astype(BF16)
    q = _dot(cqn, wqup_ref[...])
    scale = (MLA_NOPE + MLA_ROPE) ** -0.5 * LOG2E
    for hd in range(MLA_HEADS):
        lo = hd * MLA_QK_PAD
        nope = _rms(q[:, lo:lo + LANES], gqn_ref[...]) * scale
        qfull_ref[:, lo:lo + LANES] = nope.astype(BF16)
        pe = _rms(q[:, lo + LANES:lo + 2 * LANES], gqr_ref[...], MLA_ROPE)
        pe = _rope128(pe, c64, s64, MLA_ROPE // 2) * scale
        qfull_ref[:, lo + LANES:lo + 2 * LANES] = pe.astype(BF16)

    sm = _dot(hb, wsm_ref[...])
    ckv_ref[...] = _rms(sm[:, :KV_LORA], gkva_ref[...])
    kpe = _rope128(_rms(sm[:, KV_LORA:KV_LORA + LANES], gkpe_ref[...], MLA_ROPE), c64, s64,
                   MLA_ROPE // 2)
    kpe_ref[...] = kpe[:, :MLA_ROPE]
    kpeb_ref[...] = kpe.astype(BF16)
    ki = _rope128(sm[:, KV_LORA + LANES:KV_LORA + 2 * LANES], c16, s16, IDX_ROT // 2)
    ki_ref[...] = ki[:, :IDX_DIM]
    kib_ref[...] = ki.astype(BF16)
    wi = sm[:, KV_LORA + 2 * LANES:KV_LORA + 3 * LANES] * (IDX_HEADS ** -0.5)
    wit_ref[...] = wi.T[:IDX_HEADS]

    qd = _dot(hb, wqd_ref[...])
    for hd in range(DSA_HEADS):
        lo = hd * DSA_HD
        t = _rope128(_rms(qd[:, lo:lo + DSA_HD], gdq_ref[...]), c32, s32, DSA_ROT // 2)
        qd_ref[:, lo:lo + DSA_HD] = (t * (DSA_HD ** -0.5 * LOG2E)).astype(BF16)
    kvd = _dot(hb, wkvd_ref[...])
    for hd in range(DSA_KV_HEADS):
        lo = hd * DSA_HD
        t = _rope128(_rms(kvd[:, lo:lo + DSA_HD], gdk_ref[...]), c32, s32, DSA_ROT // 2)
        knew_ref[:, lo:lo + DSA_HD] = t
        knewb_ref[:, lo:lo + DSA_HD] = t.astype(BF16)
    v = kvd[:, DSA_KV_HEADS * DSA_HD:]
    vnew_ref[...] = v
    vnewb_ref[...] = v.astype(BF16)

    qi = _dot(hb, wqi_ref[...])
    for hd in range(IDX_HEADS):
        lo = hd * LANES
        t = _rope128(qi[:, lo:lo + LANES], c16, s16, IDX_ROT // 2)
        qi_ref[:, lo:lo + LANES] = (t * (IDX_DIM ** -0.5)).astype(BF16)


def _in_proj(xf, tabs, w, tm, t_len):
    n, d = xf.shape
    n_tiles = n // tm
    if t_len >= tm:
        per = t_len // tm
        tab_map = lambda i: (i % per, 0)
    else:
        tab_map = lambda i: (0, 0)
    row = lambda width: pl.BlockSpec((tm, width), lambda i: (i, 0))
    tab = pl.BlockSpec((tm, LANES), tab_map)
    weights = [w['attn_norm'], w['w_cq'], w['w_sm'], w['w_qd'], w['w_kvd'], w['w_qi'], w['w_qup'],
               w['q_a_norm'], w['kv_a_norm'], w['g_kpe'], w['g_qnope'], w['g_qrope'],
               w['dsa_q_norm'], w['dsa_k_norm']]
    out_widths = [(d, BF16), (MLA_HEADS * MLA_QK_PAD, BF16), (KV_LORA, F32), (MLA_ROPE, F32),
                  (LANES, BF16), (DSA_HEADS * DSA_HD, BF16), (DSA_KV_HEADS * DSA_HD, F32),
                  (DSA_KV_HEADS * DSA_HD, F32), (DSA_KV_HEADS * DSA_HD, BF16),
                  (DSA_KV_HEADS * DSA_HD, BF16), (IDX_HEADS * LANES, BF16), (IDX_DIM, F32),
                  (LANES, BF16)]
    return pl.pallas_call(
        _in_proj_kernel,
        grid=(n_tiles,),
        in_specs=[row(d)] + [_resident(a.shape) for a in weights] + [tab] * 6,
        out_specs=[row(wd) for wd, _ in out_widths]
        + [pl.BlockSpec((IDX_HEADS, tm), lambda i: (0, i))],
        out_shape=[jax.ShapeDtypeStruct((n, wd), dt) for wd, dt in out_widths]
        + [jax.ShapeDtypeStruct((IDX_HEADS, n), F32)],
        compiler_params=_cparams(("parallel",)),
        name="in_proj",
    )(xf, *weights, *tabs)


def _kv_up_kernel(ckv_ref, kpe_ref, w_ref, g_ref, kfull_ref, vt_ref):
    kv = _dot(ckv_ref[...].astype(BF16), w_ref[...])
    kpe = kpe_ref[...]
    g = g_ref[...]
    for hd in range(MLA_HEADS):
        lo = hd * MLA_QK_PAD
        kfull_ref[:, lo:lo + LANES] = _rms(kv[:, hd * LANES:(hd + 1) * LANES], g).astype(BF16)
        kfull_ref[:, lo + LANES:lo + 2 * LANES] = kpe
    for hd in range(MLA_HEADS):
        lo = MLA_HEADS * MLA_NOPE + hd * MLA_V
        vt_ref[0, hd] = kv[:, lo:lo + MLA_V].T.astype(BF16)


def _kv_up(ckv, kpeb, w, tm):
    n = ckv.shape[0]
    row = lambda width: pl.BlockSpec((tm, width), lambda i: (i, 0))
    return pl.pallas_call(
        _kv_up_kernel,
        grid=(n // tm,),
        in_specs=[row(KV_LORA), row(LANES), _resident(w['w_kvup'].shape),
                  _resident(w['g_knope'].shape)],
        out_specs=[row(MLA_HEADS * MLA_QK_PAD),
                   pl.BlockSpec((1, MLA_HEADS, MLA_V, tm), lambda i: (i, 0, 0, 0))],
        out_shape=[jax.ShapeDtypeStruct((n, MLA_HEADS * MLA_QK_PAD), BF16),
                   jax.ShapeDtypeStruct((n // tm, MLA_HEADS, MLA_V, tm), BF16)],
        compiler_params=_cparams(("parallel",)),
        name="kv_up",
    )(ckv, kpeb, w['w_kvup'], w['g_knope'])


def _visible_limit(q_last_pos, s_valid):
    return jnp.minimum(((q_last_pos >> CHUNK_SHIFT) + 1) * CHUNK, s_valid)


def _softmax_step(s, vt, m, l, acc):
    m_new = jnp.maximum(m, jnp.max(s, axis=0, keepdims=True))
    alpha = jnp.exp2(m - m_new)
    p = jnp.exp2(s - m_new)
    l = alpha * l + jnp.sum(p, axis=0, keepdims=True)
    acc = alpha * acc + _dot(vt, p.astype(BF16))
    return m_new, l, acc


def _run_tiles(tiles, score_fn, value_fn, carry, lookahead):
    carry = list(carry)
    pending = [score_fn(*t) for t in tiles[:lookahead]]
    for i, (stream, chunk) in enumerate(tiles):
        s = pending.pop(0)
        carry[stream] = _softmax_step(s, value_fn(stream, chunk), *carry[stream])
        if i + lookahead < len(tiles):
            pending.append(score_fn(*tiles[i + lookahead]))
    return tuple(carry)


def _softmax_init(tq, dv, streams):
    return tuple((jnp.full((1, tq), NEG_INF, F32), jnp.zeros((1, tq), F32),
                  jnp.zeros((dv, tq), F32)) for _ in range(streams))


def _lane_width(tq):
    return max(tq, LANES)


def _transposed(block):
    x = block.astype(F32)
    tq, d = x.shape
    if tq < _lane_width(tq):
        x = jnp.concatenate([x, jnp.zeros((_lane_width(tq) - tq, d), F32)], axis=0)
    return x.T.astype(BF16)


def _visibility(start, tk, qchunk, s_valid):
    kpos = start + lax.broadcasted_iota(jnp.int32, (tk, 1), 0)
    return ((kpos >> CHUNK_SHIFT) <= qchunk) & (kpos < s_valid)


def _mla_kernel(q_ref, k_ref, vt_ref, o_ref, *, tq, tk, pos0, s_valid, nh):
    qt = pl.program_id(2)
    q_first = pos0 + qt * tq
    tql = _lane_width(tq)
    qchunk = (q_first + lax.broadcasted_iota(jnp.int32, (1, tql), 1)) >> CHUNK_SHIFT
    n_chunks = pl.cdiv(_visible_limit(q_first + tq - 1, s_valid), tk)
    n_full = _visible_limit(q_first, s_valid) // tk
    qts = [_transposed(q_ref[:, hd * MLA_QK_PAD:(hd + 1) * MLA_QK_PAD]) for hd in range(nh)]

    def make_body(masked, unroll):
        def body(cu, carry):
            def score(hd, u):
                start = pl.multiple_of((cu * unroll + u) * tk, tk)
                s = _dot(k_ref[pl.ds(start, tk), hd * MLA_QK_PAD:(hd + 1) * MLA_QK_PAD], qts[hd])
                if masked:
                    s = jnp.where(_visibility(start, tk, qchunk, s_valid), s, NEG_INF)
                return s

            def value(hd, u):
                return vt_ref[cu * unroll + u, hd]

            tiles = [(hd, u) for u in range(unroll) for hd in range(nh)]
            return _run_tiles(tiles, score, value, carry, MLA_LOOKAHEAD)
        return body

    n_pairs = n_full // CHUNK_UNROLL
    carry = lax.fori_loop(0, n_pairs, make_body(False, CHUNK_UNROLL),
                          _softmax_init(tql, MLA_V, nh))
    carry = lax.fori_loop(n_pairs * CHUNK_UNROLL, n_chunks, make_body(True, 1), carry)
    for hd in range(nh):
        _, l, acc = carry[hd]
        o_ref[:, hd * MLA_V:(hd + 1) * MLA_V] = (acc / l).T[:tq].astype(BF16)


def _mla_attn(qfull, kfull, vt, b, t_len, s_pad, tq, tk, pos0, s_valid, nh):
    nq = t_len // tq
    n_ch = s_pad // tk
    kern = functools.partial(_mla_kernel, tq=tq, tk=tk, pos0=pos0, s_valid=s_valid, nh=nh)
    return pl.pallas_call(
        kern,
        grid=(b, MLA_HEADS // nh, nq),
        in_specs=[pl.BlockSpec((tq, nh * MLA_QK_PAD), lambda bi, h, i: (bi * nq + i, h)),
                  pl.BlockSpec((s_pad, nh * MLA_QK_PAD), lambda bi, h, i: (bi, h)),
                  pl.BlockSpec((n_ch, nh, MLA_V, tk), lambda bi, h, i: (bi, h, 0, 0))],
        out_specs=pl.BlockSpec((tq, nh * MLA_V), lambda bi, h, i: (bi * nq + i, h)),
        out_shape=jax.ShapeDtypeStruct((b * t_len, MLA_HEADS * MLA_V), BF16),
        compiler_params=_cparams(("parallel", "parallel", "arbitrary")),
        name="mla_attn",
    )(qfull, kfull, vt)


def _stacks_heads(tq):
    return tq < LANES and LANES % tq == 0 and IDX_HEADS % (LANES // tq) == 0


def _count_keys(preds, key_scr, n_chunks, tq, tk):
    def make_body(unroll):
        def body(cu, parts):
            parts = list(parts)
            for u in range(unroll):
                c = cu * unroll + u
                keys = key_scr[c]
                idx = c * tk + lax.broadcasted_iota(jnp.int32, (tk, 1), 0)
                for n, pred in enumerate(preds):
                    hit = jnp.where(pred(keys, idx), 1, 0).reshape(tk // COUNT_ROWS, COUNT_ROWS, tq)
                    parts[n] = parts[n] + jnp.sum(hit, axis=0)
            return tuple(parts)
        return body
    zero = jnp.zeros((COUNT_ROWS, tq), jnp.int32)
    n_pairs = n_chunks // CHUNK_UNROLL
    parts = lax.fori_loop(0, n_pairs, make_body(CHUNK_UNROLL), (zero,) * len(preds))
    parts = lax.fori_loop(n_pairs * CHUNK_UNROLL, n_chunks, make_body(1), parts)
    return [jnp.sum(p, axis=0, keepdims=True) for p in parts]


def _dsa_kernel(qi_ref, wit_ref, qd_ref, ki_ref, k_ref, vt_ref, o_ref, key_scr, bias_scr, cut_scr,
                *, tq, tk, streams, pos0, s_valid, topk, idx_bits):
    qt = pl.program_id(1)
    q_first = pos0 + qt * tq
    tql = _lane_width(tq)
    lane = lax.broadcasted_iota(jnp.int32, (1, tql), 1)
    qchunk = (q_first + lane) >> CHUNK_SHIFT
    n_chunks = pl.cdiv(_visible_limit(q_first + tq - 1, s_valid), tk)
    wit = wit_ref[0]
    heads_per_dot = LANES // tq if _stacks_heads(tq) else 1
    if heads_per_dot == 1 and tq < tql:
        wit = jnp.concatenate([wit, jnp.zeros((IDX_HEADS, tql - tq), F32)], axis=1)
    qits = []
    for h0 in range(0, IDX_HEADS, heads_per_dot):
        x = jnp.concatenate([qi_ref[:, hd * LANES:(hd + 1) * LANES]
                             for hd in range(h0, h0 + heads_per_dot)], axis=0)
        qits.append(_transposed(x))

    def score_body(c, _):
        start = pl.multiple_of(c * tk, tk)
        kic = ki_ref[pl.ds(start, tk), :]
        sc = jnp.zeros((tk, tql), F32)
        for n, qit in enumerate(qits):
            sc = sc + wit[n:n + 1, :] * jnp.maximum(_dot(kic, qit), 0.0)
        shift = tq
        while shift < tql and heads_per_dot > 1:
            sc = sc + pltpu.roll(sc, shift, 1)
            shift *= 2
        bits = pltpu.bitcast(sc + 0.0, jnp.int32)
        keys = bits ^ ((bits >> 31) & 0x7FFFFFFF)
        key_scr[c] = jnp.where(_visibility(start, tk, qchunk, s_valid), keys, INT_MIN)
        return 0
    lax.fori_loop(0, n_chunks, score_body, 0)

    def bit_body(b, t):
        cand = t + (jnp.int32(1) << (31 - b))
        cnt, = _count_keys([lambda kk, _: kk >= cand], key_scr, n_chunks, tql, tk)
        return jnp.where(cnt >= topk, cand, t)
    thr = lax.fori_loop(0, 32, bit_body, jnp.full((1, tql), INT_MIN, jnp.int32))

    cnt_gt, cnt_eq = _count_keys([lambda kk, _: kk > thr, lambda kk, _: kk == thr],
                                 key_scr, n_chunks, tql, tk)
    need = topk - cnt_gt
    cut_scr[...] = jnp.full((SUBLANES, tql), 2 ** 30, jnp.int32)
    excess = jnp.max(jnp.where((cnt_eq > need) & (thr > INT_MIN) & (lane < tq), 1, 0))

    @pl.when(excess > 0)
    def _():
        def idx_body(b, pos):
            cand = pos + (jnp.int32(1) << (idx_bits - 1 - b))
            cnt, = _count_keys([lambda kk, idx: (kk == thr) & (idx < cand)], key_scr, n_chunks,
                               tql, tk)
            return jnp.where(cnt < need, cand, pos)
        pos = lax.fori_loop(0, idx_bits, idx_body, jnp.zeros((1, tql), jnp.int32))
        cut_scr[...] = jnp.broadcast_to(pos, (SUBLANES, tql))
    cut = jnp.where(thr > INT_MIN, cut_scr[0:1, :], -1)

    def bias_body(c, _):
        keys = key_scr[c]
        idx = c * tk + lax.broadcasted_iota(jnp.int32, (tk, 1), 0)
        tie = jnp.where(idx <= cut, 0.0, NEG_INF)
        bias_scr[c] = jnp.where(keys > thr, 0.0, jnp.where(keys == thr, tie, NEG_INF))
        return 0
    lax.fori_loop(0, n_chunks, bias_body, 0)

    rep = DSA_HEADS // DSA_KV_HEADS
    for h0 in range(0, DSA_HEADS, streams):
        qts = [_transposed(qd_ref[:, hd * DSA_HD:(hd + 1) * DSA_HD]) for hd in range(h0, h0 + streams)]

        def make_att_body(unroll, qts=qts, h0=h0):
            def att_body(cu, carry):
                def score(n, u):
                    c = cu * unroll + u
                    start = pl.multiple_of(c * tk, tk)
                    g = (h0 + n) // rep
                    kc = k_ref[pl.ds(start, tk), g * DSA_HD:(g + 1) * DSA_HD]
                    return _dot(kc, qts[n]) + bias_scr[c]

                def value(n, u):
                    return vt_ref[cu * unroll + u, (h0 + n) // rep]

                tiles = [(n, u) for u in range(unroll) for n in range(streams)]
                return _run_tiles(tiles, score, value, carry, DSA_LOOKAHEAD)
            return att_body

        n_pairs = n_chunks // CHUNK_UNROLL
        carry = lax.fori_loop(0, n_pairs, make_att_body(CHUNK_UNROLL),
                              _softmax_init(tql, DSA_HD, streams))
        carry = lax.fori_loop(n_pairs * CHUNK_UNROLL, n_chunks, make_att_body(1), carry)
        for n, (_, l, acc) in enumerate(carry):
            hd = h0 + n
            o_ref[:, hd * DSA_HD:(hd + 1) * DSA_HD] = (acc / l).T[:tq].astype(BF16)


def _dsa_attn(qi, wit, qd, kib, kb, vt, b, t_len, s_pad, tq, tk, pos0, s_valid, topk):
    nq = t_len // tq
    n_ch = s_pad // tk
    idx_bits = max(1, (s_pad - 1).bit_length())
    tql = _lane_width(tq)
    if _stacks_heads(tq):
        per = LANES // tq
        wit = wit.reshape(IDX_HEADS // per, per, b * nq, tq).transpose(2, 0, 1, 3)
        wit = wit.reshape(b * nq, IDX_HEADS // per, LANES)
    else:
        wit = wit.reshape(IDX_HEADS, b * nq, tq).transpose(1, 0, 2)
    kern = functools.partial(_dsa_kernel, tq=tq, tk=tk, streams=8, pos0=pos0, s_valid=s_valid,
                             topk=topk, idx_bits=idx_bits)
    qrow = lambda width: pl.BlockSpec((tq, width), lambda bi, i: (bi * nq + i, 0))
    krow = lambda width: pl.BlockSpec((s_pad, width), lambda bi, i: (bi, 0))
    return pl.pallas_call(
        kern,
        grid=(b, nq),
        in_specs=[qrow(IDX_HEADS * LANES),
                  pl.BlockSpec((1,) + wit.shape[1:], lambda bi, i: (bi * nq + i, 0, 0)),
                  qrow(DSA_HEADS * DSA_HD), krow(LANES), krow(DSA_KV_HEADS * DSA_HD),
                  pl.BlockSpec((n_ch, DSA_KV_HEADS, DSA_HD, tk), lambda bi, i: (bi, 0, 0, 0))],
        out_specs=qrow(DSA_HEADS * DSA_HD),
        out_shape=jax.ShapeDtypeStruct((b * t_len, DSA_HEADS * DSA_HD), BF16),
        scratch_shapes=[pltpu.VMEM((n_ch, tk, tql), jnp.int32), pltpu.VMEM((n_ch, tk, tql), F32),
                        pltpu.VMEM((SUBLANES, tql), jnp.int32)],
        compiler_params=_cparams(("parallel", "arbitrary")),
        name="dsa_attn",
    )(qi, wit, qd, kib, kb, vt)


def _sigmoid(x):
    return 1.0 / (1.0 + jnp.exp(-x))


def _merge_out_kernel(h_ref, om_ref, od_ref, x_ref, wgm_ref, wgd_ref, wom_ref, wod_ref, wout_ref,
                      gffn_ref, x1_ref, hf_ref, acc_scr):
    j = pl.program_id(1)

    @pl.when(j == 0)
    def _():
        acc_scr[...] = x_ref[...]

    h = h_ref[...]
    merged = (_sigmoid(_dot(h, wgm_ref[...])) * _dot(om_ref[...], wom_ref[...])
              + _sigmoid(_dot(h, wgd_ref[...])) * _dot(od_ref[...], wod_ref[...]))
    acc_scr[...] += _dot(merged.astype(BF16), wout_ref[...])

    @pl.when(j == pl.num_programs(1) - 1)
    def _():
        x1 = acc_scr[...]
        x1_ref[...] = x1
        hf_ref[...] = _rms(x1, gffn_ref[...]).astype(BF16)


def _merge_out(h, om, od, xf, w, tm, tn):
    n, d = xf.shape
    row = lambda width: pl.BlockSpec((tm, width), lambda i, j: (i, 0))
    col = lambda rows: pl.BlockSpec((rows, tn), lambda i, j: (0, j))
    return pl.pallas_call(
        _merge_out_kernel,
        grid=(n // tm, d // tn),
        in_specs=[row(d), row(om.shape[1]), row(od.shape[1]), row(d),
                  col(d), col(d), col(om.shape[1]), col(od.shape[1]),
                  pl.BlockSpec((tn, d), lambda i, j: (j, 0)),
                  pl.BlockSpec((1, d), lambda i, j: (0, 0))],
        out_specs=[row(d), row(d)],
        out_shape=[jax.ShapeDtypeStruct((n, d), F32), jax.ShapeDtypeStruct((n, d), BF16)],
        scratch_shapes=[pltpu.VMEM((tm, d), F32)],
        compiler_params=_cparams(("parallel", "arbitrary")),
        name="merge_out",
    )(h, om, od, xf, w['w_gm'], w['w_gd'], w['w_om'], w['w_od'], w['w_out'], w['ffn_norm'])


def _gate_rows_kernel(h_ref, w_ref, o_ref):
    o_ref[...] = _dot(h_ref[...], w_ref[...])


def _gate_rows(rows, w_gate, tf):
    r, d = rows.shape
    return pl.pallas_call(
        _gate_rows_kernel,
        grid=(D_FF // tf,),
        in_specs=[pl.BlockSpec((r, d), lambda j: (0, 0)), pl.BlockSpec((d, tf), lambda j: (0, j))],
        out_specs=pl.BlockSpec((r, tf), lambda j: (0, j)),
        out_shape=jax.ShapeDtypeStruct((r, D_FF), F32),
        compiler_params=_cparams(("parallel",)),
        name="ffn_gate_rows",
    )(rows, w_gate)


def _split3(x):
    hi = x.astype(BF16)
    r1 = x - hi.astype(F32)
    mid = r1.astype(BF16)
    lo = (r1 - mid.astype(F32)).astype(BF16)
    return hi, mid, lo


def _ffn_kernel(hf_ref, x1_ref, halo_ref, wg_ref, wu_ref, wd_ref, cw_ref, cb_ref, y_ref, acc_scr,
                *, tm, t_seg, halo_rows):
    j = pl.program_id(1)

    @pl.when(j == 0)
    def _():
        acc_scr[...] = x1_ref[...]

    hf = hf_ref[...]

    r1 = lax.broadcasted_iota(jnp.int32, (tm, 1), 0) & (t_seg - 1)
    halo = halo_ref[0]
    if t_seg == tm:
        fix1 = halo[1:2, :]
        fix2 = jnp.where(r1 < 1, halo[0:1, :], halo[1:2, :])
    else:
        seg_shift = t_seg.bit_length() - 1
        row = lax.broadcasted_iota(jnp.int32, (tm, halo_rows), 0)
        colh = lax.broadcasted_iota(jnp.int32, (tm, halo_rows), 1)
        seg2 = (row >> seg_shift) * 2
        r_in = row & (t_seg - 1)
        e1 = jnp.where((r_in == 0) & (colh == seg2 + 1), 1.0, 0.0).astype(BF16)
        e2 = jnp.where(((r_in == 0) & (colh == seg2)) | ((r_in == 1) & (colh == seg2 + 1)),
                       1.0, 0.0).astype(BF16)
        pieces = _split3(halo)
        fix1 = _dot(e1, pieces[0]) + _dot(e1, pieces[1]) + _dot(e1, pieces[2])
        fix2 = _dot(e2, pieces[0]) + _dot(e2, pieces[1]) + _dot(e2, pieces[2])
    cw = cw_ref[...]
    cb = cb_ref[...]

    g = _dot(hf, wg_ref[...])
    up = _dot(hf, wu_ref[...])
    g1 = jnp.where(r1 < 1, fix1, pltpu.roll(g, 1, 0))
    g2 = jnp.where(r1 < 2, fix2, pltpu.roll(g, 2, 0))
    conv = cb + cw[0:1, :] * g2 + cw[1:2, :] * g1 + cw[2:3, :] * g
    act = conv * _sigmoid(conv) * up
    acc_scr[...] += _dot(act.astype(BF16), wd_ref[...])

    @pl.when(j == pl.num_programs(1) - 1)
    def _():
        y_ref[...] = acc_scr[...]


def _ffn(hf, x1, halo, w, tm, tf, t_seg):
    n, d = x1.shape
    halo_rows = halo.shape[1]
    kern = functools.partial(_ffn_kernel, tm=tm, t_seg=t_seg, halo_rows=halo_rows)
    row = lambda width: pl.BlockSpec((tm, width), lambda i, j: (i, 0))
    return pl.pallas_call(
        kern,
        grid=(n // tm, D_FF // tf),
        in_specs=[row(d), row(d),
                  pl.BlockSpec((1, halo_rows, tf), lambda i, j: (i, 0, j)),
                  pl.BlockSpec((d, tf), lambda i, j: (0, j)),
                  pl.BlockSpec((d, tf), lambda i, j: (0, j)),
                  pl.BlockSpec((tf, d), lambda i, j: (j, 0)),
                  pl.BlockSpec((CONV_W, tf), lambda i, j: (0, j)),
                  pl.BlockSpec((1, tf), lambda i, j: (0, j))],
        out_specs=row(d),
        out_shape=jax.ShapeDtypeStruct((n, d), F32),
        scratch_shapes=[pltpu.VMEM((tm, d), F32)],
        compiler_params=_cparams(("parallel", "arbitrary")),
        name="conv_ffn",
    )(hf, x1, halo, w['w_ffn_gate'], w['w_ffn_upv'], w['w_ffn_down'], w['conv_w'], w['conv_b'])


def _pad_cols(a, width):
    return jnp.pad(a, ((0, 0), (0, width - a.shape[1])))


def _prep_weights(attn_norm, w_in, q_a_norm, w_q_up, kv_a_norm, w_kv_up, mla_q_nope_norm,
                  mla_q_rope_norm, mla_k_nope_norm, mla_k_rope_norm, dsa_q_norm, dsa_k_norm,
                  w_o_mla, w_o_dsa, w_out, ffn_norm, w_ffn_up, conv_w, conv_b, w_ffn_down):
    d = w_in.shape[0]
    widths = (Q_LORA, KV_LORA + MLA_ROPE, DSA_HEADS * DSA_HD, 2 * DSA_KV_HEADS * DSA_HD,
              IDX_HEADS * IDX_DIM, IDX_DIM, IDX_HEADS, 2 * d)
    offs = [0]
    for wd in widths:
        offs.append(offs[-1] + wd)
    w_in = w_in.astype(BF16)
    seg = [w_in[:, offs[i]:offs[i + 1]] for i in range(len(widths))]
    w = {}
    w['w_cq'] = seg[0]
    w['w_sm'] = jnp.concatenate(
        [seg[1][:, :KV_LORA], _pad_cols(seg[1][:, KV_LORA:], LANES), _pad_cols(seg[5], LANES),
         _pad_cols(seg[6], LANES)], axis=1)
    w['w_qd'] = seg[2]
    w['w_kvd'] = seg[3]
    w['w_qi'] = jnp.pad(seg[4].reshape(d, IDX_HEADS, IDX_DIM),
                        ((0, 0), (0, 0), (0, LANES - IDX_DIM))).reshape(d, IDX_HEADS * LANES)
    w['w_gm'] = seg[7][:, :d]
    w['w_gd'] = seg[7][:, d:]
    qup = w_q_up.astype(BF16).reshape(Q_LORA, MLA_HEADS, MLA_NOPE + MLA_ROPE)
    w['w_qup'] = jnp.pad(qup, ((0, 0), (0, 0), (0, MLA_QK_PAD - MLA_NOPE - MLA_ROPE))
                         ).reshape(Q_LORA, MLA_HEADS * MLA_QK_PAD)
    kvup = w_kv_up.astype(BF16).reshape(KV_LORA, MLA_HEADS, MLA_NOPE + MLA_V)
    w['w_kvup'] = jnp.concatenate(
        [kvup[:, :, :MLA_NOPE].reshape(KV_LORA, MLA_HEADS * MLA_NOPE),
         kvup[:, :, MLA_NOPE:].reshape(KV_LORA, MLA_HEADS * MLA_V)], axis=1)
    r2 = lambda a: a.reshape(1, -1).astype(F32)
    w['attn_norm'] = r2(attn_norm)
    w['q_a_norm'] = r2(q_a_norm)
    w['kv_a_norm'] = r2(kv_a_norm)
    w['g_kpe'] = _pad_cols(r2(mla_k_rope_norm), LANES)
    w['g_qnope'] = r2(mla_q_nope_norm)
    w['g_qrope'] = _pad_cols(r2(mla_q_rope_norm), LANES)
    w['g_knope'] = r2(mla_k_nope_norm)
    w['dsa_q_norm'] = r2(dsa_q_norm)
    w['dsa_k_norm'] = r2(dsa_k_norm)
    w['ffn_norm'] = r2(ffn_norm)
    w['w_om'] = w_o_mla.astype(BF16)
    w['w_od'] = w_o_dsa.astype(BF16)
    w['w_out'] = w_out.astype(BF16)
    w['w_ffn_gate'] = w_ffn_up[:, :D_FF].astype(BF16)
    w['w_ffn_upv'] = w_ffn_up[:, D_FF:].astype(BF16)
    w['w_ffn_down'] = w_ffn_down.astype(BF16)
    w['conv_w'] = conv_w.astype(F32)
    w['conv_b'] = r2(conv_b)
    return w


def _rope_tables(pos, rot):
    half = rot // 2
    inv = ROPE_THETA ** (-jnp.arange(half, dtype=F32) / half)
    ang = pos.astype(F32)[:, None] * inv[None, :]
    cos, sin = jnp.cos(ang), jnp.sin(ang)
    t = pos.shape[0]
    c = jnp.concatenate([cos, cos, jnp.ones((t, LANES - rot), F32)], axis=1)
    s = jnp.concatenate([-sin, sin, jnp.zeros((t, LANES - rot), F32)], axis=1)
    return c, s


def _round_up(a, m):
    return (a + m - 1) // m * m


def _tile(n, target):
    t = min(n, target)
    assert n % t == 0, (n, t)
    return t


def _layer(x, pos0, past, w):
    b, t_len, d = x.shape
    n = b * t_len
    xf = x.reshape(n, d)

    tm = _tile(n, 512)
    tm_in = _tile(n, 256)
    assert tm % t_len == 0 or t_len % tm == 0
    assert tm_in % t_len == 0 or t_len % tm_in == 0
    pos = pos0 + jnp.arange(t_len, dtype=jnp.int32)
    tabs = []
    for rot in (MLA_ROPE, DSA_ROT, IDX_ROT):
        c, s = _rope_tables(pos, rot)
        if t_len < tm_in:
            c, s = jnp.tile(c, (tm_in // t_len, 1)), jnp.tile(s, (tm_in // t_len, 1))
        tabs += [c, s]

    (h, qfull, ckv, kpe, kpeb, qd, knew, vnew, knewb, vnewb, qi, ki, kib, wit) = _in_proj(
        xf, tabs, w, tm_in, t_len)

    if past is None:
        s_len = t_len
        tk = _tile(s_len, 512)
        s_pad = s_len
        all_ckv, all_kpeb, all_kb, all_vb, all_kib = ckv, kpeb, knewb, vnewb, kib
        conv_hist = jnp.zeros((b, CONV_W - 1, D_FF), F32)
    else:
        p_ckv, p_kpe, p_k, p_v, p_ki, conv_hist = past
        past_len = p_ckv.shape[1]
        s_len = past_len + t_len
        tk = 384 if s_len > 384 else _round_up(s_len, LANES)
        s_pad = _round_up(s_len, tk)
        padr = s_pad - s_len

        def cat(p_arr, new, width, dtype, lane_pad=0):
            p2 = p_arr.reshape(b, past_len, width).astype(dtype)
            if lane_pad:
                p2 = jnp.pad(p2, ((0, 0), (0, 0), (0, lane_pad)))
            a = jnp.concatenate([p2, new.reshape(b, t_len, width + lane_pad)], axis=1)
            return jnp.pad(a, ((0, 0), (0, padr), (0, 0))).reshape(b * s_pad, width + lane_pad)

        all_ckv = cat(p_ckv, ckv, KV_LORA, F32)
        all_kpeb = cat(p_kpe, kpeb, MLA_ROPE, BF16, LANES - MLA_ROPE)
        all_kb = cat(p_k, knewb, DSA_KV_HEADS * DSA_HD, BF16)
        all_vb = cat(p_v, vnewb, DSA_KV_HEADS * DSA_HD, BF16)
        all_kib = cat(p_ki, kib, IDX_DIM, BF16, LANES - IDX_DIM)

    kfull, vt_mla = _kv_up(all_ckv, all_kpeb, w, tk)
    tq = _tile(t_len, 256)
    n_ch = s_pad // tk
    vt_dsa = all_vb.reshape(b * n_ch, tk, DSA_KV_HEADS, DSA_HD).transpose(0, 2, 3, 1)
    o_mla = _mla_attn(qfull, kfull, vt_mla, b, t_len, s_pad, _tile(t_len, 512), tk, pos0, s_len, 4)
    topk = min(TOPK_MAX, s_len // 4)
    o_dsa = _dsa_attn(qi, wit, qd, all_kib, all_kb, vt_dsa, b, t_len, s_pad, _tile(t_len, 512), tk,
                      pos0, s_len, topk)

    x1, hf = _merge_out(h, o_mla, o_dsa, xf, w, tm, 512)

    t_seg = min(t_len, tm)
    assert t_seg & (t_seg - 1) == 0 and t_seg >= CONV_W - 1
    n_t = t_len // t_seg
    segs = tm // t_seg
    brows = hf.reshape(b, n_t, t_seg, d)[:, :, t_seg - (CONV_W - 1):, :].reshape(-1, d)
    bg = _gate_rows(brows, w['w_ffn_gate'], 512).reshape(b, n_t, CONV_W - 1, D_FF)
    halo = jnp.concatenate([conv_hist.astype(F32)[:, None], bg[:, :-1]], axis=1)
    halo = halo.reshape(n // tm, segs * (CONV_W - 1), D_FF)
    halo_rows = _round_up(halo.shape[1], 16)
    halo = jnp.pad(halo, ((0, 0), (0, halo_rows - halo.shape[1]), (0, 0)))
    y = _ffn(hf, x1, halo, w, tm, 512, t_seg)

    state = (ckv.reshape(b, t_len, KV_LORA), kpe.reshape(b, t_len, MLA_ROPE),
             knew.reshape(b, t_len, DSA_KV_HEADS, DSA_HD), vnew.reshape(b, t_len, DSA_KV_HEADS, DSA_HD),
             ki.reshape(b, t_len, IDX_DIM), bg[:, -1])
    return y.reshape(b, t_len, d), state


def kernel(x_prompt, x_sample, cache_mla_ckv, cache_mla_kpe, cache_dsa_k, cache_dsa_v, cache_idx_k, state_ffn_conv, attn_norm, w_in, q_a_norm, w_q_up, kv_a_norm, w_kv_up, mla_q_nope_norm, mla_q_rope_norm, mla_k_nope_norm, mla_k_rope_norm, dsa_q_norm, dsa_k_norm, w_o_mla, w_o_dsa, w_out, ffn_norm, w_ffn_up, conv_w, conv_b, w_ffn_down):
    depth = w_in.shape[0]
    past_len = cache_mla_ckv.shape[2]
    xp, xs = x_prompt, x_sample
    p_states, s_states = [], []
    for l in range(depth):
        w = _prep_weights(attn_norm[l], w_in[l], q_a_norm[l], w_q_up[l], kv_a_norm[l], w_kv_up[l],
                          mla_q_nope_norm[l], mla_q_rope_norm[l], mla_k_nope_norm[l],
                          mla_k_rope_norm[l], dsa_q_norm[l], dsa_k_norm[l], w_o_mla[l], w_o_dsa[l],
                          w_out[l], ffn_norm[l], w_ffn_up[l], conv_w[l], conv_b[l], w_ffn_down[l])
        past = (cache_mla_ckv[l], cache_mla_kpe[l], cache_dsa_k[l], cache_dsa_v[l],
                cache_idx_k[l], state_ffn_conv[l])
        xp, sp = _layer(xp, 0, None, w)
        xs, ss = _layer(xs, past_len, past, w)
        p_states.append(sp)
        s_states.append(ss)
    p_out = [jnp.stack(a) for a in zip(*p_states)]
    s_out = [jnp.stack(a) for a in zip(*s_states)]
    return (xp, xs, *p_out, *s_out)
```
